```python
import jax, jax.numpy as jnp
from jax import lax
import numpy as np

D_MODEL = 1024
BATCH = 16
SEQ = 2048
DEPTH = 1

GLA_HEADS = 4
GLA_DK = 128
GLA_DV = 256
GLA_GATE_RANK = 16
GLA_GATE_NORM = 16.0
GLA_CHUNK = 64
SWA_HEADS = 16
SWA_KV_HEADS = 4
SWA_HEAD_DIM = 64
SWA_WINDOW = 128
SWA_BLOCK = 128
ROPE_THETA = 500000.0
ROPE_DIM = SWA_HEAD_DIM // 4
PEER_HEADS = 8
PEER_NKEYS = 128
PEER_EXPERTS = PEER_NKEYS * PEER_NKEYS
PEER_QDIM = 256
PEER_TOPK = 16
PEER_TOKEN_BLOCK = 128
NORM_EPS = 1e-6

GLA_KW = GLA_HEADS * GLA_DK
GLA_VW = GLA_HEADS * GLA_DV
SWA_QW = SWA_HEADS * SWA_HEAD_DIM
SWA_KVW = SWA_KV_HEADS * SWA_HEAD_DIM
IN_SPLITS = (GLA_KW, GLA_KW, GLA_VW, GLA_VW, GLA_GATE_RANK, SWA_QW, SWA_KVW, SWA_KVW, D_MODEL, D_MODEL)
IN_WIDTH = sum(IN_SPLITS)
IN_SPLIT_POINTS = tuple(int(v) for v in np.cumsum(IN_SPLITS)[:-1])

kernel_name = 'hybrid_gla_swa_sink_peer'


def rmsnorm(x, g):
    xf = x.astype(jnp.float32)
    y = xf * lax.rsqrt(jnp.mean(xf * xf, axis=-1, keepdims=True) + NORM_EPS)
    return (y * g.astype(jnp.float32)).astype(x.dtype)


def rope_tables(seqlen):
    pos = jnp.arange(seqlen, dtype=jnp.float32)
    inv_freq = ROPE_THETA ** (-jnp.arange(0, ROPE_DIM, 2, dtype=jnp.float32) / ROPE_DIM)
    ang = pos[:, None] * inv_freq[None, :]
    return jnp.cos(ang), jnp.sin(ang)


def partial_rope(x, cos, sin):
    half = ROPE_DIM // 2
    xf = x.astype(jnp.float32)
    x1, x2, rest = xf[..., :half], xf[..., half:ROPE_DIM], xf[..., ROPE_DIM:]
    c, s = cos[:, None, :], sin[:, None, :]
    return jnp.concatenate([x1 * c - x2 * s, x2 * c + x1 * s, rest], axis=-1).astype(x.dtype)


def gla_chunked(q, k, v, log_a):
    bsz, seqlen, nh, dk = q.shape
    dv = v.shape[-1]
    c = GLA_CHUNK
    n = seqlen // c

    def to_chunks(t):
        return t.astype(jnp.float32).reshape(bsz, n, c, nh, t.shape[-1]).transpose(0, 3, 1, 2, 4)

    qc = to_chunks(q) * (dk ** -0.5)
    kc, vc, ac = to_chunks(k), to_chunks(v), to_chunks(log_a)
    b = jnp.cumsum(ac, axis=3)
    b_ref = b[:, :, :, c // 2:c // 2 + 1]
    b_last = b[:, :, :, -1:]
    causal = jnp.tril(jnp.ones((c, c), dtype=bool))
    att = jnp.einsum('bhntd,bhnsd->bhnts', qc * jnp.exp(b - b_ref), kc * jnp.exp(b_ref - b))
    att = jnp.where(causal, att, 0.0)
    o_intra = jnp.einsum('bhnts,bhnse->bhnte', att, vc)
    q_inter = qc * jnp.exp(b)
    k_end = kc * jnp.exp(b_last - b)
    decay_end = jnp.exp(b_last[:, :, :, 0])

    def step(state, xs):
        qi, ki, vi, di = xs
        out = jnp.einsum('bhtd,bhde->bhte', qi, state)
        state = di[..., None] * state + jnp.einsum('bhsd,bhse->bhde', ki, vi)
        return state, out

    state0 = jnp.zeros((bsz, nh, dk, dv), jnp.float32)
    xs = (jnp.moveaxis(q_inter, 2, 0), jnp.moveaxis(k_end, 2, 0), jnp.moveaxis(vc, 2, 0), jnp.moveaxis(decay_end, 2, 0))
    _, o_inter = lax.scan(step, state0, xs)
    o = o_intra + jnp.moveaxis(o_inter, 0, 2)
    return o.transpose(0, 2, 3, 1, 4).reshape(bsz, seqlen, nh, dv)


def swa_sink_attention(q, k, v, sinks):
    bsz, seqlen, nh, hd = q.shape
    kvh = k.shape[2]
    grp = nh // kvh
    L = SWA_BLOCK
    n = seqlen // L
    qb = q.reshape(bsz, n, L, kvh, grp, hd)
    kb = k.reshape(bsz, n, L, kvh, hd)
    vb = v.reshape(bsz, n, L, kvh, hd)
    kk = jnp.concatenate([jnp.concatenate([jnp.zeros_like(kb[:, :1]), kb[:, :-1]], axis=1), kb], axis=2)
    vv = jnp.concatenate([jnp.concatenate([jnp.zeros_like(vb[:, :1]), vb[:, :-1]], axis=1), vb], axis=2)
    s = jnp.einsum('bnqhgd,bnkhd->bnhgqk', qb, kk).astype(jnp.float32) * (hd ** -0.5)
    qpos = jnp.arange(L)[:, None] + L
    kpos = jnp.arange(2 * L)[None, :]
    rel = qpos - kpos
    band = (rel >= 0) & (rel < SWA_WINDOW)
    valid = band[None] & ((jnp.arange(n)[:, None, None] > 0) | (kpos[None] >= L))
    s = jnp.where(valid[None, :, None, None], s, -jnp.inf)
    sink = sinks.astype(jnp.float32).reshape(kvh, grp)[None, None, :, :, None, None]
    m = jnp.maximum(jnp.max(s, axis=-1, keepdims=True), sink)
    p = jnp.exp(s - m)
    denom = jnp.sum(p, axis=-1, keepdims=True) + jnp.exp(sink - m)
    o = jnp.einsum('bnhgqk,bnkhd->bnqhgd', (p / denom).astype(v.dtype), vv)
    return o.reshape(bsz, seqlen, nh, hd)


def peer_layer(xn, w_pq, sub_keys, expert_u, expert_v):
    bsz, seqlen, d = xn.shape
    t = bsz * seqlen
    xt = xn.reshape(t, d)
    qp = (xt @ w_pq).reshape(t, PEER_HEADS, 2, PEER_QDIM // 2)
    scores = jnp.einsum('thpd,hpkd->thpk', qp, sub_keys).astype(jnp.float32)
    s_top, i_top = lax.top_k(scores, PEER_TOPK)
    cand = (s_top[:, :, 0, :, None] + s_top[:, :, 1, None, :]).reshape(t, PEER_HEADS, PEER_TOPK * PEER_TOPK)
    cand_idx = (i_top[:, :, 0, :, None] * PEER_NKEYS + i_top[:, :, 1, None, :]).reshape(t, PEER_HEADS, PEER_TOPK * PEER_TOPK)
    best, pos = lax.top_k(cand, PEER_TOPK)
    idx = jnp.take_along_axis(cand_idx, pos, axis=-1)
    gate = jax.nn.softmax(best, axis=-1).astype(xn.dtype)
    hk = PEER_HEADS * PEER_TOPK
    nblk = t // PEER_TOKEN_BLOCK

    def expert_block(args):
        xb, ib, gb = args
        u = expert_u[ib]
        h = jax.nn.gelu(jnp.einsum('lkd,ld->lk', u, xb), approximate=False)
        return jnp.einsum('lk,lkd->ld', gb * h, expert_v[ib])

    out = lax.map(expert_block, (xt.reshape(nblk, PEER_TOKEN_BLOCK, d),
                                 idx.reshape(nblk, PEER_TOKEN_BLOCK, hk),
                                 gate.reshape(nblk, PEER_TOKEN_BLOCK, hk)))
    return out.reshape(bsz, seqlen, d)


def setup_inputs(seed: int = 0) -> dict:
    key = jax.random.key(seed)
    ks = jax.random.split(key, 17)
    f32 = jnp.float32
    nrm = lambda k, shape, scale: jax.random.normal(k, shape, f32) * scale
    return {
        'x': nrm(ks[0], (BATCH, SEQ, D_MODEL), 1.0),
        'norm_mix_g': 1.0 + nrm(ks[1], (DEPTH, D_MODEL), 0.02),
        'w_in': nrm(ks[2], (DEPTH, D_MODEL, IN_WIDTH), D_MODEL ** -0.5),
        'w_gk2': nrm(ks[3], (DEPTH, GLA_GATE_RANK, GLA_KW), GLA_GATE_RANK ** -0.5),
        'b_gk': nrm(ks[4], (DEPTH, GLA_KW), 0.1),
        'gla_norm_g': 1.0 + nrm(ks[5], (DEPTH, GLA_DV), 0.02),
        'q_norm_g': 1.0 + nrm(ks[6], (DEPTH, SWA_HEAD_DIM), 0.02),
        'k_norm_g': 1.0 + nrm(ks[7], (DEPTH, SWA_HEAD_DIM), 0.02),
        'attn_sinks': nrm(ks[8], (DEPTH, SWA_HEADS), 0.1),
        'w_branch_a': nrm(ks[9], (DEPTH, GLA_VW, D_MODEL), GLA_VW ** -0.5),
        'w_branch_b': nrm(ks[10], (DEPTH, SWA_QW, D_MODEL), SWA_QW ** -0.5),
        'w_out': nrm(ks[11], (DEPTH, D_MODEL, D_MODEL), D_MODEL ** -0.5),
        'norm_ffn_g': 1.0 + nrm(ks[12], (DEPTH, D_MODEL), 0.02),
        'w_peer_q': nrm(ks[13], (DEPTH, D_MODEL, PEER_HEADS * PEER_QDIM), D_MODEL ** -0.5),
        'peer_sub_keys': nrm(ks[14], (DEPTH, PEER_HEADS, 2, PEER_NKEYS, PEER_QDIM // 2), (PEER_QDIM // 2) ** -0.5),
        'peer_u': nrm(ks[15], (DEPTH, PEER_EXPERTS, D_MODEL), D_MODEL ** -0.5),
        'peer_v': nrm(ks[16], (DEPTH, PEER_EXPERTS, D_MODEL), D_MODEL ** -0.5),
    }


def reference(x, norm_mix_g, w_in, w_gk2, b_gk, gla_norm_g, q_norm_g, k_norm_g, attn_sinks,
              w_branch_a, w_branch_b, w_out, norm_ffn_g, w_peer_q, peer_sub_keys, peer_u, peer_v):
    bsz, seqlen, d = x.shape
    cos, sin = rope_tables(seqlen)
    for l in range(DEPTH):
        h = rmsnorm(x, norm_mix_g[l])
        proj = h @ w_in[l]
        gq, gk, gv, gr, glr, sq, sk, sv, gate_a, gate_b = jnp.split(proj, IN_SPLIT_POINTS, axis=-1)
        log_a = jax.nn.log_sigmoid((glr @ w_gk2[l] + b_gk[l]).astype(jnp.float32)) / GLA_GATE_NORM
        o_a = gla_chunked(gq.reshape(bsz, seqlen, GLA_HEADS, GLA_DK),
                          gk.reshape(bsz, seqlen, GLA_HEADS, GLA_DK),
                          gv.reshape(bsz, seqlen, GLA_HEADS, GLA_DV),
                          log_a.reshape(bsz, seqlen, GLA_HEADS, GLA_DK))
        o_a = rmsnorm(o_a, gla_norm_g[l]).astype(x.dtype) * jax.nn.silu(gr.reshape(bsz, seqlen, GLA_HEADS, GLA_DV))
        y_a = o_a.reshape(bsz, seqlen, GLA_VW) @ w_branch_a[l]
        q = partial_rope(rmsnorm(sq.reshape(bsz, seqlen, SWA_HEADS, SWA_HEAD_DIM), q_norm_g[l]), cos, sin)
        k = partial_rope(rmsnorm(sk.reshape(bsz, seqlen, SWA_KV_HEADS, SWA_HEAD_DIM), k_norm_g[l]), cos, sin)
        o_b = swa_sink_attention(q, k, sv.reshape(bsz, seqlen, SWA_KV_HEADS, SWA_HEAD_DIM), attn_sinks[l])
        y_b = o_b.reshape(bsz, seqlen, SWA_QW) @ w_branch_b[l]
        merged = jax.nn.sigmoid(gate_a) * y_a + jax.nn.sigmoid(gate_b) * y_b
        x = x + merged @ w_out[l]
        x = x + peer_layer(rmsnorm(x, norm_ffn_g[l]), w_peer_q[l], peer_sub_keys[l], peer_u[l], peer_v[l])
    return x
```

```python
import functools

import jax
import jax.numpy as jnp
import numpy as np
from jax import lax
from jax.experimental import pallas as pl
from jax.experimental.pallas import tpu as pltpu

D_MODEL = 1024
GLA_HEADS = 4
GLA_DK = 128
GLA_DV = 256
GLA_GATE_RANK = 16
GLA_GATE_NORM = 16.0
GLA_CHUNK = 64
SWA_HEADS = 16
SWA_KV_HEADS = 4
SWA_HEAD_DIM = 64
SWA_WINDOW = 128
SWA_BLOCK = 128
ROPE_THETA = 500000.0
ROPE_DIM = SWA_HEAD_DIM // 4
PEER_HEADS = 8
PEER_NKEYS = 128
PEER_QDIM = 256
PEER_TOPK = 16
PEER_TOKEN_BLOCK = 128
NORM_EPS = 1e-6

GLA_KW = GLA_HEADS * GLA_DK
GLA_VW = GLA_HEADS * GLA_DV
SWA_QW = SWA_HEADS * SWA_HEAD_DIM
SWA_KVW = SWA_KV_HEADS * SWA_HEAD_DIM
IN_SPLITS = (GLA_KW, GLA_KW, GLA_VW, GLA_VW, GLA_GATE_RANK, SWA_QW, SWA_KVW, SWA_KVW, D_MODEL, D_MODEL)
IN_SPLIT_POINTS = tuple(int(v) for v in np.cumsum(IN_SPLITS)[:-1])


def _norm_matmul_kernel(x_ref, g_ref, w_ref, o_ref):
    x = x_ref[...]
    y = x * lax.rsqrt(jnp.mean(x * x, axis=-1, keepdims=True) + NORM_EPS) * g_ref[...]
    o_ref[...] = jnp.dot(y.astype(jnp.bfloat16), w_ref[...], preferred_element_type=jnp.float32)


def norm_matmul(x, g, w_bf16, tm=512, tn=512):
    t, d = x.shape
    n = w_bf16.shape[1]
    return pl.pallas_call(
        _norm_matmul_kernel,
        grid=(t // tm, pl.cdiv(n, tn)),
        in_specs=[pl.BlockSpec((tm, d), lambda i, j: (i, 0)),
                  pl.BlockSpec((1, d), lambda i, j: (0, 0)),
                  pl.BlockSpec((d, tn), lambda i, j: (0, j))],
        out_specs=pl.BlockSpec((tm, tn), lambda i, j: (i, j)),
        out_shape=jax.ShapeDtypeStruct((t, n), jnp.float32),
        name="norm_in_proj",
    )(x, g.reshape(1, d), w_bf16)


def _rmsnorm(x, g):
    return x * lax.rsqrt(jnp.mean(x * x, axis=-1, keepdims=True) + NORM_EPS) * g


def _rope_tables(seqlen):
    pos = jnp.arange(seqlen, dtype=jnp.float32)
    inv_freq = ROPE_THETA ** (-jnp.arange(0, ROPE_DIM, 2, dtype=jnp.float32) / ROPE_DIM)
    ang = pos[:, None] * inv_freq[None, :]
    return jnp.cos(ang), jnp.sin(ang)


def _partial_rope(x, cos, sin):
    half = ROPE_DIM // 2
    x1, x2, rest = x[..., :half], x[..., half:ROPE_DIM], x[..., ROPE_DIM:]
    c, s = cos[:, None, :], sin[:, None, :]
    return jnp.concatenate([x1 * c - x2 * s, x2 * c + x1 * s, rest], axis=-1)


def _gla_chunked(q, k, v, log_a):
    bsz, seqlen, nh, dk = q.shape
    dv = v.shape[-1]
    c = GLA_CHUNK
    n = seqlen // c

    def to_chunks(t):
        return t.reshape(bsz, n, c, nh, t.shape[-1]).transpose(0, 3, 1, 2, 4)

    qc = to_chunks(q) * (dk ** -0.5)
    kc, vc, ac = to_chunks(k), to_chunks(v), to_chunks(log_a)
    b = jnp.cumsum(ac, axis=3)
    b_ref = b[:, :, :, c // 2:c // 2 + 1]
    b_last = b[:, :, :, -1:]
    causal = jnp.tril(jnp.ones((c, c), dtype=bool))
    att = jnp.einsum('bhntd,bhnsd->bhnts', qc * jnp.exp(b - b_ref), kc * jnp.exp(b_ref - b))
    att = jnp.where(causal, att, 0.0)
    o_intra = jnp.einsum('bhnts,bhnse->bhnte', att, vc)
    q_inter = qc * jnp.exp(b)
    k_end = kc * jnp.exp(b_last - b)
    decay_end = jnp.exp(b_last[:, :, :, 0])

    def step(state, xs):
        qi, ki, vi, di = xs
        out = jnp.einsum('bhtd,bhde->bhte', qi, state)
        state = di[..., None] * state + jnp.einsum('bhsd,bhse->bhde', ki, vi)
        return state, out

    state0 = jnp.zeros((bsz, nh, dk, dv), jnp.float32)
    xs = (jnp.moveaxis(q_inter, 2, 0), jnp.moveaxis(k_end, 2, 0), jnp.moveaxis(vc, 2, 0), jnp.moveaxis(decay_end, 2, 0))
    _, o_inter = lax.scan(step, state0, xs)
    o = o_intra + jnp.moveaxis(o_inter, 0, 2)
    return o.transpose(0, 2, 3, 1, 4).reshape(bsz, seqlen, nh, dv)


def _swa(q, k, v, sinks):
    bsz, seqlen, nh, hd = q.shape
    kvh = k.shape[2]
    grp = nh // kvh
    L = SWA_BLOCK
    n = seqlen // L
    qb = q.reshape(bsz, n, L, kvh, grp, hd)
    kb = k.reshape(bsz, n, L, kvh, hd)
    vb = v.reshape(bsz, n, L, kvh, hd)
    kk = jnp.concatenate([jnp.concatenate([jnp.zeros_like(kb[:, :1]), kb[:, :-1]], axis=1), kb], axis=2)
    vv = jnp.concatenate([jnp.concatenate([jnp.zeros_like(vb[:, :1]), vb[:, :-1]], axis=1), vb], axis=2)
    s = jnp.einsum('bnqhgd,bnkhd->bnhgqk', qb, kk) * (hd ** -0.5)
    qpos = jnp.arange(L)[:, None] + L
    kpos = jnp.arange(2 * L)[None, :]
    rel = qpos - kpos
    band = (rel >= 0) & (rel < SWA_WINDOW)
    valid = band[None] & ((jnp.arange(n)[:, None, None] > 0) | (kpos[None] >= L))
    s = jnp.where(valid[None, :, None, None], s, -jnp.inf)
    sink = sinks.reshape(kvh, grp)[None, None, :, :, None, None]
    m = jnp.maximum(jnp.max(s, axis=-1, keepdims=True), sink)
    p = jnp.exp(s - m)
    denom = jnp.sum(p, axis=-1, keepdims=True) + jnp.exp(sink - m)
    o = jnp.einsum('bnhgqk,bnkhd->bnqhgd', p / denom, vv)
    return o.reshape(bsz, seqlen, nh, hd)


def _peer(xn, w_pq, sub_keys, expert_u, expert_v):
    bsz, seqlen, d = xn.shape
    t = bsz * seqlen
    xt = xn.reshape(t, d)
    qp = (xt @ w_pq).reshape(t, PEER_HEADS, 2, PEER_QDIM // 2)
    scores = jnp.einsum('thpd,hpkd->thpk', qp, sub_keys)
    s_top, i_top = lax.top_k(scores, PEER_TOPK)
    cand = (s_top[:, :, 0, :, None] + s_top[:, :, 1, None, :]).reshape(t, PEER_HEADS, PEER_TOPK * PEER_TOPK)
    cand_idx = (i_top[:, :, 0, :, None] * PEER_NKEYS + i_top[:, :, 1, None, :]).reshape(t, PEER_HEADS, PEER_TOPK * PEER_TOPK)
    best, pos = lax.top_k(cand, PEER_TOPK)
    idx = jnp.take_along_axis(cand_idx, pos, axis=-1)
    gate = jax.nn.softmax(best, axis=-1)
    hk = PEER_HEADS * PEER_TOPK
    nblk = t // PEER_TOKEN_BLOCK

    def expert_block(args):
        xb, ib, gb = args
        u = expert_u[ib]
        h = jax.nn.gelu(jnp.einsum('lkd,ld->lk', u, xb), approximate=False)
        return jnp.einsum('lk,lkd->ld', gb * h, expert_v[ib])

    out = lax.map(expert_block, (xt.reshape(nblk, PEER_TOKEN_BLOCK, d),
                                 idx.reshape(nblk, PEER_TOKEN_BLOCK, hk),
                                 gate.reshape(nblk, PEER_TOKEN_BLOCK, hk)))
    return out.reshape(bsz, seqlen, d)


def kernel(x, norm_mix_g, w_in, w_gk2, b_gk, gla_norm_g, q_norm_g, k_norm_g, attn_sinks, w_branch_a, w_branch_b, w_out, norm_ffn_g, w_peer_q, peer_sub_keys, peer_u, peer_v):
    bsz, seqlen, d = x.shape
    t = bsz * seqlen
    cos, sin = _rope_tables(seqlen)
    l = 0
    proj = norm_matmul(x.reshape(t, d), norm_mix_g[l], w_in[l].astype(jnp.bfloat16)).reshape(bsz, seqlen, -1)
    gq, gk, gv, gr, glr, sq, sk, sv, gate_a, gate_b = jnp.split(proj, IN_SPLIT_POINTS, axis=-1)
    log_a = jax.nn.log_sigmoid(glr @ w_gk2[l] + b_gk[l]) / GLA_GATE_NORM
    o_a = _gla_chunked(gq.reshape(bsz, seqlen, GLA_HEADS, GLA_DK),
                       gk.reshape(bsz, seqlen, GLA_HEADS, GLA_DK),
                       gv.reshape(bsz, seqlen, GLA_HEADS, GLA_DV),
                       log_a.reshape(bsz, seqlen, GLA_HEADS, GLA_DK))
    o_a = _rmsnorm(o_a, gla_norm_g[l]) * jax.nn.silu(gr.reshape(bsz, seqlen, GLA_HEADS, GLA_DV))
    y_a = o_a.reshape(bsz, seqlen, GLA_VW) @ w_branch_a[l]
    q = _partial_rope(_rmsnorm(sq.reshape(bsz, seqlen, SWA_HEADS, SWA_HEAD_DIM), q_norm_g[l]), cos, sin)
    k = _partial_rope(_rmsnorm(sk.reshape(bsz, seqlen, SWA_KV_HEADS, SWA_HEAD_DIM), k_norm_g[l]), cos, sin)
    o_b = _swa(q, k, sv.reshape(bsz, seqlen, SWA_KV_HEADS, SWA_HEAD_DIM), attn_sinks[l])
    y_b = o_b.reshape(bsz, seqlen, SWA_QW) @ w_branch_b[l]
    merged = jax.nn.sigmoid(gate_a) * y_a + jax.nn.sigmoid(gate_b) * y_b
    x = x + merged @ w_out[l]
    x = x + _peer(_rmsnorm(x, norm_ffn_g[l]), w_peer_q[l], peer_sub_keys[l], peer_u[l], peer_v[l])
    return x
```

```python
import functools

import jax
import jax.numpy as jnp
import numpy as np
from jax import lax
from jax.experimental import pallas as pl
from jax.experimental.pallas import tpu as pltpu

D_MODEL = 1024
GLA_HEADS = 4
GLA_DK = 128
GLA_DV = 256
GLA_GATE_RANK = 16
GLA_GATE_NORM = 16.0
GLA_CHUNK = 64
SWA_HEADS = 16
SWA_KV_HEADS = 4
SWA_GROUP = SWA_HEADS // SWA_KV_HEADS
SWA_HEAD_DIM = 64
SWA_WINDOW = 128
SWA_BLOCK = 128
ROPE_THETA = 500000.0
ROPE_DIM = SWA_HEAD_DIM // 4
ROPE_HALF = ROPE_DIM // 2
PEER_HEADS = 8
PEER_NKEYS = 128
PEER_QDIM = 256
PEER_TOPK = 16
PEER_HK = PEER_HEADS * PEER_TOPK
NORM_EPS = 1e-6

GLA_KW = GLA_HEADS * GLA_DK
GLA_VW = GLA_HEADS * GLA_DV
SWA_QW = SWA_HEADS * SWA_HEAD_DIM
SWA_KVW = SWA_KV_HEADS * SWA_HEAD_DIM
IN_SPLITS = (GLA_KW, GLA_KW, GLA_VW, GLA_VW, GLA_GATE_RANK, SWA_QW, SWA_KVW, SWA_KVW, D_MODEL, D_MODEL)
IN_OFFSETS = tuple(int(v) for v in np.cumsum((0,) + IN_SPLITS)[:-1])

LANES = 128
COL_GQ = 0
COL_GK = COL_GQ + GLA_KW
COL_GV = COL_GK + GLA_KW
COL_GR = COL_GV + GLA_VW
COL_SQ = COL_GR + GLA_VW
COL_SK = COL_SQ + SWA_QW
COL_SV = COL_SK + SWA_QW
COL_GA = COL_SV + SWA_QW
COL_GB = COL_GA + D_MODEL
COL_GLR = COL_GB + D_MODEL
PROJ_W = COL_GLR + LANES

VMEM_LIMIT = 48 * 1024 * 1024

F32 = jnp.float32
BF16 = jnp.bfloat16
HIGHEST = lax.Precision.HIGHEST


def _dot(a, b, precision=None):
    return jnp.dot(a, b, preferred_element_type=F32, precision=precision)


def _dot_nt(a, b):
    return lax.dot_general(a, b, (((1,), (1,)), ((), ())), preferred_element_type=F32)


def _gelu_exact(x):
    return 0.5 * x * (1.0 + lax.erf(x * (2.0 ** -0.5)))


def _dot_tn(a, b):
    return lax.dot_general(a, b, (((0,), (0,)), ((), ())), preferred_element_type=F32)


def _in_proj_kernel(x_ref, g_ref, w_ref, o_ref, xn_ref):
    @pl.when(pl.program_id(1) == 0)
    def _():
        x = x_ref[...]
        y = x * lax.rsqrt(jnp.mean(x * x, axis=-1, keepdims=True) + NORM_EPS) * g_ref[...]
        xn_ref[...] = y.astype(BF16)

    o_ref[...] = _dot(xn_ref[...], w_ref[...])


def in_proj(x2d, g, w_bf16, tm=1024, tn=640):
    t, d = x2d.shape
    n = w_bf16.shape[1]
    return pl.pallas_call(
        _in_proj_kernel,
        grid=(t // tm, n // tn),
        in_specs=[pl.BlockSpec((tm, d), lambda i, j: (i, 0)),
                  pl.BlockSpec((1, d), lambda i, j: (0, 0)),
                  pl.BlockSpec((d, tn), lambda i, j: (0, j))],
        out_specs=pl.BlockSpec((tm, tn), lambda i, j: (i, j)),
        out_shape=jax.ShapeDtypeStruct((t, n), F32),
        scratch_shapes=[pltpu.VMEM((tm, d), BF16)],
        compiler_params=pltpu.CompilerParams(
            dimension_semantics=("parallel", "arbitrary"), vmem_limit_bytes=VMEM_LIMIT),
        name="in_proj",
    )(x2d, g.reshape(1, d), w_bf16)


def _relayout_w_in(w_in):
    seg = [w_in[:, o:o + s] for o, s in zip(IN_OFFSETS, IN_SPLITS)]
    gq, gk, gv, gr, glr, sq, sk, sv, ga, gb = seg
    d = w_in.shape[0]

    def rep(w):
        w = w.reshape(d, SWA_KV_HEADS, 1, SWA_HEAD_DIM)
        return jnp.broadcast_to(w, (d, SWA_KV_HEADS, SWA_GROUP, SWA_HEAD_DIM)).reshape(d, SWA_QW)

    glr_pad = jnp.pad(glr, ((0, 0), (0, LANES - GLA_GATE_RANK)))
    return jnp.concatenate([gq, gk, gv, gr, sq, rep(sk), rep(sv), ga, gb, glr_pad], axis=1).astype(BF16)


GLA_STEP = 256


def _gla_kernel(q_ref, k_ref, v_ref, gr_ref, glr_ref, w2_ref, b2_ref, g_ref, o_ref, st_ref):
    @pl.when(pl.program_id(2) == 0)
    def _():
        st_ref[...] = jnp.zeros_like(st_ref)

    c = GLA_CHUNK
    row = lax.broadcasted_iota(jnp.int32, (c, c), 0)
    col = lax.broadcasted_iota(jnp.int32, (c, c), 1)
    causal = col <= row
    tril = causal.astype(F32)
    w2 = w2_ref[...]
    b2 = b2_ref[...]
    gain = g_ref[...]
    for ci in range(GLA_STEP // c):
        sl = pl.ds(ci * c, c)
        q = q_ref[sl, :] * (GLA_DK ** -0.5)
        k = k_ref[sl, :]
        v = v_ref[sl, :].astype(BF16)
        z = _dot(glr_ref[sl, :], w2, HIGHEST) + b2
        log_a = jax.nn.log_sigmoid(z) / GLA_GATE_NORM
        b = _dot(tril, log_a, HIGHEST)
        b_mid = b[c // 2:c // 2 + 1, :]
        b_last = b[c - 1:c, :]
        att = _dot_nt((q * jnp.exp(b - b_mid)).astype(BF16), (k * jnp.exp(b_mid - b)).astype(BF16))
        att = jnp.where(causal, att, 0.0)
        o = _dot(att.astype(BF16), v)
        st = st_ref[...]
        o = o + _dot_nt((q * jnp.exp(b)).astype(BF16), st.astype(BF16))
        k_end = (k * jnp.exp(b_last - b)).astype(BF16)
        st_ref[...] = st * jnp.exp(b_last) + _dot_tn(v, k_end)
        y = o * lax.rsqrt(jnp.mean(o * o, axis=-1, keepdims=True) + NORM_EPS) * gain
        o_ref[sl, :] = (y * jax.nn.silu(gr_ref[sl, :])).astype(o_ref.dtype)


def gla(proj, w_gk2_pad, b_gk, gla_norm_g, bsz, seqlen):
    t = bsz * seqlen
    ns = seqlen // GLA_STEP
    rowmap = lambda b, h, s: b * ns + s
    kq, kk = COL_GQ // GLA_DK, COL_GK // GLA_DK
    kv, kr = COL_GV // GLA_DV, COL_GR // GLA_DV
    kl = COL_GLR // LANES
    return pl.pallas_call(
        _gla_kernel,
        grid=(bsz, GLA_HEADS, ns),
        in_specs=[pl.BlockSpec((GLA_STEP, GLA_DK), lambda b, h, s: (rowmap(b, h, s), kq + h)),
                  pl.BlockSpec((GLA_STEP, GLA_DK), lambda b, h, s: (rowmap(b, h, s), kk + h)),
                  pl.BlockSpec((GLA_STEP, GLA_DV), lambda b, h, s: (rowmap(b, h, s), kv + h)),
                  pl.BlockSpec((GLA_STEP, GLA_DV), lambda b, h, s: (rowmap(b, h, s), kr + h)),
                  pl.BlockSpec((GLA_STEP, LANES), lambda b, h, s: (rowmap(b, h, s), kl)),
                  pl.BlockSpec((LANES, GLA_DK), lambda b, h, s: (0, h)),
                  pl.BlockSpec((1, GLA_DK), lambda b, h, s: (0, h)),
                  pl.BlockSpec((1, GLA_DV), lambda b, h, s: (0, 0))],
        out_specs=pl.BlockSpec((GLA_STEP, GLA_DV), lambda b, h, s: (rowmap(b, h, s), h)),
        out_shape=jax.ShapeDtypeStruct((t, GLA_VW), BF16),
        scratch_shapes=[pltpu.VMEM((GLA_DV, GLA_DK), F32)],
        compiler_params=pltpu.CompilerParams(
            dimension_semantics=("parallel", "parallel", "arbitrary"), vmem_limit_bytes=VMEM_LIMIT),
        name="gla",
    )(proj, proj, proj, proj, proj, w_gk2_pad, b_gk.reshape(1, GLA_KW), gla_norm_g.reshape(1, GLA_DV))


SWA_GW = SWA_GROUP * SWA_HEAD_DIM


def _swa_kernel(sinks_ref, q_ref, k_ref, v_ref, cos_ref, sin_ref, qg_ref, kg_ref, bd_ref, o_ref,
                kprev_ref, vprev_ref):
    n = pl.program_id(1)

    @pl.when(n == 0)
    def _():
        kprev_ref[...] = jnp.zeros_like(kprev_ref)
        vprev_ref[...] = jnp.zeros_like(vprev_ref)

    L = SWA_BLOCK
    cos = cos_ref[...]
    sin = sin_ref[...]
    bd = bd_ref[...]
    lane = lax.broadcasted_iota(jnp.int32, (L, SWA_GW), 1)
    seg = lane // SWA_HEAD_DIM
    first_half = (lane % SWA_HEAD_DIM) < ROPE_HALF

    def norm_rope(x, gain):
        ms = _dot(x * x, bd, HIGHEST) * (1.0 / SWA_HEAD_DIM)
        y = x * lax.rsqrt(ms + NORM_EPS) * gain
        partner = jnp.where(first_half, pltpu.roll(y, SWA_GW - ROPE_HALF, 1), pltpu.roll(y, ROPE_HALF, 1))
        return y * cos + partner * sin

    qi = lax.broadcasted_iota(jnp.int32, (L, L), 0)
    ki = lax.broadcasted_iota(jnp.int32, (L, L), 1)
    mask_cur = ki <= qi
    mask_prev = ki > qi + jnp.where(n > 0, 0, L)
    neg_inf = F32(-jnp.inf)
    for j in range(SWA_KV_HEADS):
        cs = pl.ds(j * SWA_GW, SWA_GW)
        q = (norm_rope(q_ref[:, cs], qg_ref[...]) * (SWA_HEAD_DIM ** -0.5)).astype(BF16)
        k_cur = norm_rope(k_ref[:, cs], kg_ref[...]).astype(BF16)
        v_cur = v_ref[:, cs].astype(BF16)
        k_prev = kprev_ref[:, cs]
        v_prev = vprev_ref[:, cs]
        acc = jnp.zeros((L, SWA_GW), F32)
        for g in range(SWA_GROUP):
            qm = jnp.where(seg == g, q, jnp.zeros_like(q))
            s_cur = jnp.where(mask_cur, _dot_nt(qm, k_cur), neg_inf)
            s_prev = jnp.where(mask_prev, _dot_nt(qm, k_prev), neg_inf)
            sink = sinks_ref[j * SWA_GROUP + g]
            m = jnp.maximum(jnp.maximum(jnp.max(s_cur, axis=-1, keepdims=True),
                                        jnp.max(s_prev, axis=-1, keepdims=True)), sink)
            p_cur = jnp.exp(s_cur - m)
            p_prev = jnp.exp(s_prev - m)
            denom = (jnp.sum(p_cur, axis=-1, keepdims=True) + jnp.sum(p_prev, axis=-1, keepdims=True)
                     + jnp.exp(sink - m))
            og = (_dot(p_cur.astype(BF16), v_cur) + _dot(p_prev.astype(BF16), v_prev)) / denom
            acc = acc + jnp.where(seg == g, og, 0.0)
        o_ref[:, cs] = acc.astype(o_ref.dtype)
        kprev_ref[:, cs] = k_cur
        vprev_ref[:, cs] = v_cur


def _rope_tables(seqlen):
    pos = jnp.arange(seqlen, dtype=F32)
    inv_freq = ROPE_THETA ** (-jnp.arange(0, ROPE_DIM, 2, dtype=F32) / ROPE_DIM)
    ang = pos[:, None] * inv_freq[None, :]
    cos, sin = jnp.cos(ang), jnp.sin(ang)
    rest = SWA_HEAD_DIM - ROPE_DIM
    cos_h = jnp.concatenate([cos, cos, jnp.ones((seqlen, rest), F32)], axis=1)
    sin_h = jnp.concatenate([-sin, sin, jnp.zeros((seqlen, rest), F32)], axis=1)
    return jnp.tile(cos_h, (1, SWA_GROUP)), jnp.tile(sin_h, (1, SWA_GROUP))


def swa(proj, q_norm_g, k_norm_g, sinks, bsz, seqlen):
    t = bsz * seqlen
    nb = seqlen // SWA_BLOCK
    cos_t, sin_t = _rope_tables(seqlen)
    head = np.arange(SWA_GW) // SWA_HEAD_DIM
    blockdiag = jnp.asarray((head[:, None] == head[None, :]).astype(np.float32))
    rowmap = lambda b, n: b * nb + n
    cq, ck, cv = COL_SQ // SWA_QW, COL_SK // SWA_QW, COL_SV // SWA_QW
    grid_spec = pltpu.PrefetchScalarGridSpec(
        num_scalar_prefetch=1,
        grid=(bsz, nb),
        in_specs=[pl.BlockSpec((SWA_BLOCK, SWA_QW), lambda b, n, s: (rowmap(b, n), cq)),
                  pl.BlockSpec((SWA_BLOCK, SWA_QW), lambda b, n, s: (rowmap(b, n), ck)),
                  pl.BlockSpec((SWA_BLOCK, SWA_QW), lambda b, n, s: (rowmap(b, n), cv)),
                  pl.BlockSpec((SWA_BLOCK, SWA_GW), lambda b, n, s: (n, 0)),
                  pl.BlockSpec((SWA_BLOCK, SWA_GW), lambda b, n, s: (n, 0)),
                  pl.BlockSpec((1, SWA_GW), lambda b, n, s: (0, 0)),
                  pl.BlockSpec((1, SWA_GW), lambda b, n, s: (0, 0)),
                  pl.BlockSpec((SWA_GW, SWA_GW), lambda b, n, s: (0, 0))],
        out_specs=pl.BlockSpec((SWA_BLOCK, SWA_QW), lambda b, n, s: (rowmap(b, n), 0)),
        scratch_shapes=[pltpu.VMEM((SWA_BLOCK, SWA_QW), BF16), pltpu.VMEM((SWA_BLOCK, SWA_QW), BF16)],
    )
    return pl.pallas_call(
        _swa_kernel,
        grid_spec=grid_spec,
        out_shape=jax.ShapeDtypeStruct((t, SWA_QW), BF16),
        compiler_params=pltpu.CompilerParams(
            dimension_semantics=("parallel", "arbitrary"), vmem_limit_bytes=VMEM_LIMIT),
        name="swa",
    )(sinks, proj, proj, proj, cos_t, sin_t,
      jnp.tile(q_norm_g, SWA_GROUP).reshape(1, SWA_GW), jnp.tile(k_norm_g, SWA_GROUP).reshape(1, SWA_GW),
      blockdiag)


def _merge_kernel(oa_ref, ob_ref, ga_ref, gb_ref, x_ref, wa_ref, wb_ref, wo_ref, g_ref, wq_ref,
                  x1_ref, xn_ref, qp_ref):
    y_a = _dot(oa_ref[...], wa_ref[...])
    y_b = _dot(ob_ref[...], wb_ref[...])
    merged = jax.nn.sigmoid(ga_ref[...]) * y_a + jax.nn.sigmoid(gb_ref[...]) * y_b
    x1 = x_ref[...] + _dot(merged.astype(BF16), wo_ref[...])
    x1_ref[...] = x1
    xn = x1 * lax.rsqrt(jnp.mean(x1 * x1, axis=-1, keepdims=True) + NORM_EPS) * g_ref[...]
    xn_ref[...] = xn
    qp_ref[...] = _dot(xn.astype(BF16), wq_ref[...]).astype(qp_ref.dtype)


def merge(o_a, o_b, proj, x2d, w_a, w_b, w_o, norm_g, w_pq, tm=256):
    t, d = x2d.shape
    nq = w_pq.shape[1]
    ca, cb = COL_GA // D_MODEL, COL_GB // D_MODEL
    full = lambda shape: pl.BlockSpec(shape, lambda i: (0, 0))
    return pl.pallas_call(
        _merge_kernel,
        grid=(t // tm,),
        in_specs=[pl.BlockSpec((tm, GLA_VW), lambda i: (i, 0)),
                  pl.BlockSpec((tm, SWA_QW), lambda i: (i, 0)),
                  pl.BlockSpec((tm, d), lambda i: (i, ca)),
                  pl.BlockSpec((tm, d), lambda i: (i, cb)),
                  pl.BlockSpec((tm, d), lambda i: (i, 0)),
                  full((GLA_VW, d)), full((SWA_QW, d)), full((d, d)), full((1, d)), full((d, nq))],
        out_specs=[pl.BlockSpec((tm, d), lambda i: (i, 0)),
                   pl.BlockSpec((tm, d), lambda i: (i, 0)),
                   pl.BlockSpec((tm, nq), lambda i: (i, 0))],
        out_shape=[jax.ShapeDtypeStruct((t, d), F32),
                   jax.ShapeDtypeStruct((t, d), F32),
                   jax.ShapeDtypeStruct((t, nq), BF16)],
        compiler_params=pltpu.CompilerParams(
            dimension_semantics=("parallel",), vmem_limit_bytes=VMEM_LIMIT),
        name="merge",
    )(o_a, o_b, proj, proj, x2d, w_a.astype(BF16), w_b.astype(BF16), w_o.astype(BF16),
      norm_g.reshape(1, d), w_pq.astype(BF16))


PEER_TOK = LANES
PEER_HALF = PEER_QDIM // 2
PEER_NCAND = PEER_TOPK * PEER_TOPK


def _top_rows(s, payload=None):
    r = s.shape[0]
    rid = lax.broadcasted_iota(jnp.int32, s.shape, 0).astype(F32)
    vals, rows = [], []
    for _ in range(PEER_TOPK):
        m = jnp.max(s, axis=0, keepdims=True)
        first = jnp.min(jnp.where(s == m, rid, F32(r)), axis=0, keepdims=True)
        hit = rid == first
        vals.append(m)
        if payload is None:
            rows.append(first)
        else:
            rows.append(jnp.max(jnp.where(hit, payload, -1.0), axis=0, keepdims=True))
        s = jnp.where(hit, -jnp.inf, s)
    return jnp.concatenate(vals, axis=0), jnp.concatenate(rows, axis=0)


def _peer_topk_kernel(qp_ref, keys_ref, idx_ref, gate_ref):
    idx_rows, gate_rows = [], []
    for h in range(PEER_HEADS):
        tops = []
        for p in range(2):
            qseg = qp_ref[:, pl.ds((h * 2 + p) * PEER_HALF, PEER_HALF)]
            s = _dot_nt(keys_ref[h * 2 + p], qseg)
            tops.append(_top_rows(s))
        (v1, i1), (v2, i2) = tops
        cand = jnp.concatenate([v1[a:a + 1, :] + v2 for a in range(PEER_TOPK)], axis=0)
        cidx = jnp.concatenate([i1[a:a + 1, :] * PEER_NKEYS + i2 for a in range(PEER_TOPK)], axis=0)
        best, eidx = _top_rows(cand, cidx)
        e = jnp.exp(best - best[0:1, :])
        gate_rows.append(e / jnp.sum(e, axis=0, keepdims=True))
        idx_rows.append(eidx)
    gate = jnp.concatenate(gate_rows, axis=0)
    idx = jnp.concatenate(idx_rows, axis=0)
    gate_ref[...] = gate.T
    idx_ref[...] = idx.T.astype(jnp.int32)


def peer_topk(qp, sub_keys):
    t = qp.shape[0]
    keys = sub_keys.reshape(PEER_HEADS * 2, PEER_NKEYS, PEER_HALF).astype(BF16)
    return pl.pallas_call(
        _peer_topk_kernel,
        grid=(t // PEER_TOK,),
        in_specs=[pl.BlockSpec((PEER_TOK, PEER_HEADS * PEER_QDIM), lambda i: (i, 0)),
                  pl.BlockSpec((PEER_HEADS * 2, PEER_NKEYS, PEER_HALF), lambda i: (0, 0, 0))],
        out_specs=[pl.BlockSpec((PEER_TOK, PEER_HK), lambda i: (i, 0)),
                   pl.BlockSpec((PEER_TOK, PEER_HK), lambda i: (i, 0))],
        out_shape=[jax.ShapeDtypeStruct((t, PEER_HK), jnp.int32),
                   jax.ShapeDtypeStruct((t, PEER_HK), F32)],
        compiler_params=pltpu.CompilerParams(
            dimension_semantics=("parallel",), vmem_limit_bytes=VMEM_LIMIT),
        name="peer_topk",
    )(qp, keys)


PEXP_TOK = 64
PEXP_NBUF = 3


def _peer_exp_kernel(idx_ref, xn_ref, gate_ref, x1_ref, uv_ref, o_ref, buf_ref, sem_ref):
    d = D_MODEL

    def row_copy(tok, k, slot):
        return pltpu.make_async_copy(uv_ref.at[idx_ref[tok, k]], buf_ref.at[slot, k], sem_ref.at[slot])

    def start(tok, slot):
        for k in range(PEER_HK):
            row_copy(tok, k, slot).start()

    def wait(slot):
        for k in range(PEER_HK):
            pltpu.make_async_copy(uv_ref.at[0], buf_ref.at[slot, k], sem_ref.at[slot]).wait()

    for s in range(PEXP_NBUF - 1):
        start(s, s)

    def body(tok, carry):
        slot = tok % PEXP_NBUF
        nxt = tok + (PEXP_NBUF - 1)

        @pl.when(nxt < PEXP_TOK)
        def _():
            start(nxt, nxt % PEXP_NBUF)

        wait(slot)
        u = buf_ref[slot, :, pl.ds(0, d)].astype(BF16)
        v = buf_ref[slot, :, pl.ds(d, d)].astype(BF16)
        x = xn_ref[pl.ds(tok, 1), :].astype(BF16)
        h = _dot_nt(x, u)
        w = gate_ref[pl.ds(tok, 1), :] * _gelu_exact(h)
        o_ref[pl.ds(tok, 1), :] = x1_ref[pl.ds(tok, 1), :] + _dot(w.astype(BF16), v)
        return carry

    lax.fori_loop(0, PEXP_TOK, body, 0)


def peer_exp(idx, xn, gate, x1, uv):
    t, d = xn.shape
    return pl.pallas_call(
        _peer_exp_kernel,
        grid=(t // PEXP_TOK,),
        in_specs=[pl.BlockSpec((PEXP_TOK, PEER_HK), lambda i: (i, 0), memory_space=pltpu.SMEM),
                  pl.BlockSpec((PEXP_TOK, d), lambda i: (i, 0)),
                  pl.BlockSpec((PEXP_TOK, PEER_HK), lambda i: (i, 0)),
                  pl.BlockSpec((PEXP_TOK, d), lambda i: (i, 0)),
                  pl.BlockSpec(memory_space=pl.ANY)],
        out_specs=pl.BlockSpec((PEXP_TOK, d), lambda i: (i, 0)),
        out_shape=jax.ShapeDtypeStruct((t, d), F32),
        scratch_shapes=[pltpu.VMEM((PEXP_NBUF, PEER_HK, 2 * d), F32),
                        pltpu.SemaphoreType.DMA((PEXP_NBUF,))],
        compiler_params=pltpu.CompilerParams(
            dimension_semantics=("parallel",), vmem_limit_bytes=VMEM_LIMIT),
        name="peer_exp",
    )(idx, xn, gate, x1, uv)


def kernel(x, norm_mix_g, w_in, w_gk2, b_gk, gla_norm_g, q_norm_g, k_norm_g, attn_sinks, w_branch_a, w_branch_b, w_out, norm_ffn_g, w_peer_q, peer_sub_keys, peer_u, peer_v):
    bsz, seqlen, d = x.shape
    t = bsz * seqlen
    depth = w_in.shape[0]
    x2d = x.reshape(t, d)
    for l in range(depth):
        proj = in_proj(x2d, norm_mix_g[l], _relayout_w_in(w_in[l]))
        w_gk2_pad = jnp.pad(w_gk2[l], ((0, LANES - GLA_GATE_RANK), (0, 0)))
        o_a = gla(proj, w_gk2_pad, b_gk[l], gla_norm_g[l], bsz, seqlen)
        o_b = swa(proj, q_norm_g[l], k_norm_g[l], attn_sinks[l], bsz, seqlen)
        x1, xn, qp = merge(o_a, o_b, proj, x2d, w_branch_a[l], w_branch_b[l], w_out[l],
                           norm_ffn_g[l], w_peer_q[l])
        idx, gate = peer_topk(qp, peer_sub_keys[l])
        uv = jnp.concatenate([peer_u[l], peer_v[l]], axis=1)
        x2d = peer_exp(idx, xn, gate, x1, uv)
    return x2d.reshape(bsz, seqlen, d)
```

```python
import functools

import jax
import jax.numpy as jnp
import numpy as np
from jax import lax
from jax.experimental import pallas as pl
from jax.experimental.pallas import tpu as pltpu

D_MODEL = 1024
GLA_HEADS = 4
GLA_DK = 128
GLA_DV = 256
GLA_GATE_RANK = 16
GLA_GATE_NORM = 16.0
GLA_CHUNK = 64
SWA_HEADS = 16
SWA_KV_HEADS = 4
SWA_GROUP = SWA_HEADS // SWA_KV_HEADS
SWA_HEAD_DIM = 64
SWA_WINDOW = 128
SWA_BLOCK = 128
ROPE_THETA = 500000.0
ROPE_DIM = SWA_HEAD_DIM // 4
ROPE_HALF = ROPE_DIM // 2
PEER_HEADS = 8
PEER_NKEYS = 128
PEER_QDIM = 256
PEER_TOPK = 16
PEER_HK = PEER_HEADS * PEER_TOPK
NORM_EPS = 1e-6

GLA_KW = GLA_HEADS * GLA_DK
GLA_VW = GLA_HEADS * GLA_DV
SWA_QW = SWA_HEADS * SWA_HEAD_DIM
SWA_KVW = SWA_KV_HEADS * SWA_HEAD_DIM
IN_SPLITS = (GLA_KW, GLA_KW, GLA_VW, GLA_VW, GLA_GATE_RANK, SWA_QW, SWA_KVW, SWA_KVW, D_MODEL, D_MODEL)
IN_OFFSETS = tuple(int(v) for v in np.cumsum((0,) + IN_SPLITS)[:-1])

LANES = 128
COL_GQ = 0
COL_GK = COL_GQ + GLA_KW
COL_GV = COL_GK + GLA_KW
COL_GR = COL_GV + GLA_VW
COL_SQ = COL_GR + GLA_VW
COL_SK = COL_SQ + SWA_QW
COL_SV = COL_SK + SWA_QW
COL_GA = COL_SV + SWA_QW
COL_GB = COL_GA + D_MODEL
COL_GLR = COL_GB + D_MODEL
PROJ_W = COL_GLR + LANES

VMEM_LIMIT = 48 * 1024 * 1024

F32 = jnp.float32
BF16 = jnp.bfloat16
HIGHEST = lax.Precision.HIGHEST


def _dot(a, b, precision=None):
    return jnp.dot(a, b, preferred_element_type=F32, precision=precision)


def _dot_nt(a, b):
    return lax.dot_general(a, b, (((1,), (1,)), ((), ())), preferred_element_type=F32)


def _gelu_exact(x):
    return 0.5 * x * (1.0 + lax.erf(x * (2.0 ** -0.5)))


def _dot_tn(a, b):
    return lax.dot_general(a, b, (((0,), (0,)), ((), ())), preferred_element_type=F32)


def _in_proj_kernel(x_ref, g_ref, w_ref, o_ref, xn_ref):
    @pl.when(pl.program_id(1) == 0)
    def _():
        x = x_ref[...]
        y = x * lax.rsqrt(jnp.mean(x * x, axis=-1, keepdims=True) + NORM_EPS) * g_ref[...]
        xn_ref[...] = y.astype(BF16)

    o_ref[...] = _dot(xn_ref[...], w_ref[...])


def in_proj(x2d, g, w_bf16, tm=1024, tn=640):
    t, d = x2d.shape
    n = w_bf16.shape[1]
    return pl.pallas_call(
        _in_proj_kernel,
        grid=(t // tm, n // tn),
        in_specs=[pl.BlockSpec((tm, d), lambda i, j: (i, 0)),
                  pl.BlockSpec((1, d), lambda i, j: (0, 0)),
                  pl.BlockSpec((d, tn), lambda i, j: (0, j))],
        out_specs=pl.BlockSpec((tm, tn), lambda i, j: (i, j)),
        out_shape=jax.ShapeDtypeStruct((t, n), F32),
        scratch_shapes=[pltpu.VMEM((tm, d), BF16)],
        compiler_params=pltpu.CompilerParams(
            dimension_semantics=("parallel", "arbitrary"), vmem_limit_bytes=VMEM_LIMIT),
        name="in_proj",
    )(x2d, g.reshape(1, d), w_bf16)


def _relayout_w_in(w_in):
    seg = [w_in[:, o:o + s] for o, s in zip(IN_OFFSETS, IN_SPLITS)]
    gq, gk, gv, gr, glr, sq, sk, sv, ga, gb = seg
    d = w_in.shape[0]

    def rep(w):
        w = w.reshape(d, SWA_KV_HEADS, 1, SWA_HEAD_DIM)
        return jnp.broadcast_to(w, (d, SWA_KV_HEADS, SWA_GROUP, SWA_HEAD_DIM)).reshape(d, SWA_QW)

    glr_pad = jnp.pad(glr, ((0, 0), (0, LANES - GLA_GATE_RANK)))
    return jnp.concatenate([gq, gk, gv, gr, sq, rep(sk), rep(sv), ga, gb, glr_pad], axis=1).astype(BF16)


GLA_STEP = 256


def _gla_kernel(q_ref, k_ref, v_ref, gr_ref, glr_ref, w2_ref, b2_ref, g_ref, o_ref, st_ref):
    @pl.when(pl.program_id(2) == 0)
    def _():
        st_ref[...] = jnp.zeros_like(st_ref)

    c = GLA_CHUNK
    row = lax.broadcasted_iota(jnp.int32, (c, c), 0)
    col = lax.broadcasted_iota(jnp.int32, (c, c), 1)
    causal = col <= row
    tril = causal.astype(F32)
    w2 = w2_ref[...]
    b2 = b2_ref[...]
    gain = g_ref[...]
    for ci in range(GLA_STEP // c):
        sl = pl.ds(ci * c, c)
        q = q_ref[sl, :] * (GLA_DK ** -0.5)
        k = k_ref[sl, :]
        v = v_ref[sl, :].astype(BF16)
        z = _dot(glr_ref[sl, :], w2, HIGHEST) + b2
        log_a = jax.nn.log_sigmoid(z) / GLA_GATE_NORM
        b = _dot(tril, log_a, HIGHEST)
        b_mid = b[c // 2:c // 2 + 1, :]
        b_last = b[c - 1:c, :]
        att = _dot_nt((q * jnp.exp(b - b_mid)).astype(BF16), (k * jnp.exp(b_mid - b)).astype(BF16))
        att = jnp.where(causal, att, 0.0)
        o = _dot(att.astype(BF16), v)
        st = st_ref[...]
        o = o + _dot_nt((q * jnp.exp(b)).astype(BF16), st.astype(BF16))
        k_end = (k * jnp.exp(b_last - b)).astype(BF16)
        st_ref[...] = st * jnp.exp(b_last) + _dot_tn(v, k_end)
        y = o * lax.rsqrt(jnp.mean(o * o, axis=-1, keepdims=True) + NORM_EPS) * gain
        o_ref[sl, :] = (y * jax.nn.silu(gr_ref[sl, :])).astype(o_ref.dtype)


def gla(proj, w_gk2_pad, b_gk, gla_norm_g, bsz, seqlen):
    t = bsz * seqlen
    ns = seqlen // GLA_STEP
    rowmap = lambda b, h, s: b * ns + s
    kq, kk = COL_GQ // GLA_DK, COL_GK // GLA_DK
    kv, kr = COL_GV // GLA_DV, COL_GR // GLA_DV
    kl = COL_GLR // LANES
    return pl.pallas_call(
        _gla_kernel,
        grid=(bsz, GLA_HEADS, ns),
        in_specs=[pl.BlockSpec((GLA_STEP, GLA_DK), lambda b, h, s: (rowmap(b, h, s), kq + h)),
                  pl.BlockSpec((GLA_STEP, GLA_DK), lambda b, h, s: (rowmap(b, h, s), kk + h)),
                  pl.BlockSpec((GLA_STEP, GLA_DV), lambda b, h, s: (rowmap(b, h, s), kv + h)),
                  pl.BlockSpec((GLA_STEP, GLA_DV), lambda b, h, s: (rowmap(b, h, s), kr + h)),
                  pl.BlockSpec((GLA_STEP, LANES), lambda b, h, s: (rowmap(b, h, s), kl)),
                  pl.BlockSpec((LANES, GLA_DK), lambda b, h, s: (0, h)),
                  pl.BlockSpec((1, GLA_DK), lambda b, h, s: (0, h)),
                  pl.BlockSpec((1, GLA_DV), lambda b, h, s: (0, 0))],
        out_specs=pl.BlockSpec((GLA_STEP, GLA_DV), lambda b, h, s: (rowmap(b, h, s), h)),
        out_shape=jax.ShapeDtypeStruct((t, GLA_VW), BF16),
        scratch_shapes=[pltpu.VMEM((GLA_DV, GLA_DK), F32)],
        compiler_params=pltpu.CompilerParams(
            dimension_semantics=("parallel", "parallel", "arbitrary"), vmem_limit_bytes=VMEM_LIMIT),
        name="gla",
    )(proj, proj, proj, proj, proj, w_gk2_pad, b_gk.reshape(1, GLA_KW), gla_norm_g.reshape(1, GLA_DV))


SWA_GW = SWA_GROUP * SWA_HEAD_DIM


def _swa_kernel(sinks_ref, q_ref, k_ref, v_ref, cos_ref, sin_ref, qg_ref, kg_ref, bd_ref, o_ref,
                kprev_ref, vprev_ref):
    n = pl.program_id(1)

    @pl.when(n == 0)
    def _():
        kprev_ref[...] = jnp.zeros_like(kprev_ref)
        vprev_ref[...] = jnp.zeros_like(vprev_ref)

    L = SWA_BLOCK
    cos = cos_ref[...]
    sin = sin_ref[...]
    bd = bd_ref[...]
    lane = lax.broadcasted_iota(jnp.int32, (L, SWA_GW), 1)
    seg = lane // SWA_HEAD_DIM
    first_half = (lane % SWA_HEAD_DIM) < ROPE_HALF

    def norm_rope(x, gain):
        ms = _dot(x * x, bd, HIGHEST) * (1.0 / SWA_HEAD_DIM)
        y = x * lax.rsqrt(ms + NORM_EPS) * gain
        partner = jnp.where(first_half, pltpu.roll(y, SWA_GW - ROPE_HALF, 1), pltpu.roll(y, ROPE_HALF, 1))
        return y * cos + partner * sin

    qi = lax.broadcasted_iota(jnp.int32, (L, L), 0)
    ki = lax.broadcasted_iota(jnp.int32, (L, L), 1)
    mask_cur = ki <= qi
    mask_prev = ki > qi + jnp.where(n > 0, 0, L)
    neg_inf = F32(-jnp.inf)
    for j in range(SWA_KV_HEADS):
        cs = pl.ds(j * SWA_GW, SWA_GW)
        q = (norm_rope(q_ref[:, cs], qg_ref[...]) * (SWA_HEAD_DIM ** -0.5)).astype(BF16)
        k_cur = norm_rope(k_ref[:, cs], kg_ref[...]).astype(BF16)
        v_cur = v_ref[:, cs].astype(BF16)
        k_prev = kprev_ref[:, cs]
        v_prev = vprev_ref[:, cs]
        acc = jnp.zeros((L, SWA_GW), F32)
        for g in range(SWA_GROUP):
            qm = jnp.where(seg == g, q, jnp.zeros_like(q))
            s_cur = jnp.where(mask_cur, _dot_nt(qm, k_cur), neg_inf)
            s_prev = jnp.where(mask_prev, _dot_nt(qm, k_prev), neg_inf)
            sink = sinks_ref[j * SWA_GROUP + g]
            m = jnp.maximum(jnp.maximum(jnp.max(s_cur, axis=-1, keepdims=True),
                                        jnp.max(s_prev, axis=-1, keepdims=True)), sink)
            p_cur = jnp.exp(s_cur - m)
            p_prev = jnp.exp(s_prev - m)
            denom = (jnp.sum(p_cur, axis=-1, keepdims=True) + jnp.sum(p_prev, axis=-1, keepdims=True)
                     + jnp.exp(sink - m))
            og = (_dot(p_cur.astype(BF16), v_cur) + _dot(p_prev.astype(BF16), v_prev)) / denom
            acc = acc + jnp.where(seg == g, og, 0.0)
        o_ref[:, cs] = acc.astype(o_ref.dtype)
        kprev_ref[:, cs] = k_cur
        vprev_ref[:, cs] = v_cur


def _rope_tables(seqlen):
    pos = jnp.arange(seqlen, dtype=F32)
    inv_freq = ROPE_THETA ** (-jnp.arange(0, ROPE_DIM, 2, dtype=F32) / ROPE_DIM)
    ang = pos[:, None] * inv_freq[None, :]
    cos, sin = jnp.cos(ang), jnp.sin(ang)
    rest = SWA_HEAD_DIM - ROPE_DIM
    cos_h = jnp.concatenate([cos, cos, jnp.ones((seqlen, rest), F32)], axis=1)
    sin_h = jnp.concatenate([-sin, sin, jnp.zeros((seqlen, rest), F32)], axis=1)
    return jnp.tile(cos_h, (1, SWA_GROUP)), jnp.tile(sin_h, (1, SWA_GROUP))


def swa(proj, q_norm_g, k_norm_g, sinks, bsz, seqlen):
    t = bsz * seqlen
    nb = seqlen // SWA_BLOCK
    cos_t, sin_t = _rope_tables(seqlen)
    head = np.arange(SWA_GW) // SWA_HEAD_DIM
    blockdiag = jnp.asarray((head[:, None] == head[None, :]).astype(np.float32))
    rowmap = lambda b, n: b * nb + n
    cq, ck, cv = COL_SQ // SWA_QW, COL_SK // SWA_QW, COL_SV // SWA_QW
    grid_spec = pltpu.PrefetchScalarGridSpec(
        num_scalar_prefetch=1,
        grid=(bsz, nb),
        in_specs=[pl.BlockSpec((SWA_BLOCK, SWA_QW), lambda b, n, s: (rowmap(b, n), cq)),
                  pl.BlockSpec((SWA_BLOCK, SWA_QW), lambda b, n, s: (rowmap(b, n), ck)),
                  pl.BlockSpec((SWA_BLOCK, SWA_QW), lambda b, n, s: (rowmap(b, n), cv)),
                  pl.BlockSpec((SWA_BLOCK, SWA_GW), lambda b, n, s: (n, 0)),
                  pl.BlockSpec((SWA_BLOCK, SWA_GW), lambda b, n, s: (n, 0)),
                  pl.BlockSpec((1, SWA_GW), lambda b, n, s: (0, 0)),
                  pl.BlockSpec((1, SWA_GW), lambda b, n, s: (0, 0)),
                  pl.BlockSpec((SWA_GW, SWA_GW), lambda b, n, s: (0, 0))],
        out_specs=pl.BlockSpec((SWA_BLOCK, SWA_QW), lambda b, n, s: (rowmap(b, n), 0)),
        scratch_shapes=[pltpu.VMEM((SWA_BLOCK, SWA_QW), BF16), pltpu.VMEM((SWA_BLOCK, SWA_QW), BF16)],
    )
    return pl.pallas_call(
        _swa_kernel,
        grid_spec=grid_spec,
        out_shape=jax.ShapeDtypeStruct((t, SWA_QW), BF16),
        compiler_params=pltpu.CompilerParams(
            dimension_semantics=("parallel", "arbitrary"), vmem_limit_bytes=VMEM_LIMIT),
        name="swa",
    )(sinks, proj, proj, proj, cos_t, sin_t,
      jnp.tile(q_norm_g, SWA_GROUP).reshape(1, SWA_GW), jnp.tile(k_norm_g, SWA_GROUP).reshape(1, SWA_GW),
      blockdiag)


def _merge_kernel(oa_ref, ob_ref, ga_ref, gb_ref, x_ref, wa_ref, wb_ref, wo_ref, g_ref, wq_ref,
                  x1_ref, xn_ref, qp_ref):
    y_a = _dot(oa_ref[...], wa_ref[...])
    y_b = _dot(ob_ref[...], wb_ref[...])
    merged = jax.nn.sigmoid(ga_ref[...]) * y_a + jax.nn.sigmoid(gb_ref[...]) * y_b
    x1 = x_ref[...] + _dot(merged.astype(BF16), wo_ref[...])
    x1_ref[...] = x1
    xn = x1 * lax.rsqrt(jnp.mean(x1 * x1, axis=-1, keepdims=True) + NORM_EPS) * g_ref[...]
    xn_ref[...] = xn
    qp_ref[...] = _dot(xn.astype(BF16), wq_ref[...]).astype(qp_ref.dtype)


def merge(o_a, o_b, proj, x2d, w_a, w_b, w_o, norm_g, w_pq, tm=256):
    t, d = x2d.shape
    nq = w_pq.shape[1]
    ca, cb = COL_GA // D_MODEL, COL_GB // D_MODEL
    full = lambda shape: pl.BlockSpec(shape, lambda i: (0, 0))
    return pl.pallas_call(
        _merge_kernel,
        grid=(t // tm,),
        in_specs=[pl.BlockSpec((tm, GLA_VW), lambda i: (i, 0)),
                  pl.BlockSpec((tm, SWA_QW), lambda i: (i, 0)),
                  pl.BlockSpec((tm, d), lambda i: (i, ca)),
                  pl.BlockSpec((tm, d), lambda i: (i, cb)),
                  pl.BlockSpec((tm, d), lambda i: (i, 0)),
                  full((GLA_VW, d)), full((SWA_QW, d)), full((d, d)), full((1, d)), full((d, nq))],
        out_specs=[pl.BlockSpec((tm, d), lambda i: (i, 0)),
                   pl.BlockSpec((tm, d), lambda i: (i, 0)),
                   pl.BlockSpec((tm, nq), lambda i: (i, 0))],
        out_shape=[jax.ShapeDtypeStruct((t, d), F32),
                   jax.ShapeDtypeStruct((t, d), F32),
                   jax.ShapeDtypeStruct((t, nq), BF16)],
        compiler_params=pltpu.CompilerParams(
            dimension_semantics=("parallel",), vmem_limit_bytes=VMEM_LIMIT),
        name="merge",
    )(o_a, o_b, proj, proj, x2d, w_a.astype(BF16), w_b.astype(BF16), w_o.astype(BF16),
      norm_g.reshape(1, d), w_pq.astype(BF16))


PEER_TOK = LANES
PEER_HALF = PEER_QDIM // 2
PEER_NCAND = PEER_TOPK * PEER_TOPK


def _top_rows(s, payload=None):
    r = s.shape[0]
    rid = lax.broadcasted_iota(jnp.int32, s.shape, 0).astype(F32)
    vals, rows = [], []
    for _ in range(PEER_TOPK):
        m = jnp.max(s, axis=0, keepdims=True)
        first = jnp.min(jnp.where(s == m, rid, F32(r)), axis=0, keepdims=True)
        hit = rid == first
        vals.append(m)
        if payload is None:
            rows.append(first)
        else:
            rows.append(jnp.max(jnp.where(hit, payload, -1.0), axis=0, keepdims=True))
        s = jnp.where(hit, -jnp.inf, s)
    return jnp.concatenate(vals, axis=0), jnp.concatenate(rows, axis=0)


def _peer_topk_kernel(qp_ref, keys_ref, idx_ref, gate_ref):
    idx_rows, gate_rows = [], []
    for h in range(PEER_HEADS):
        tops = []
        for p in range(2):
            qseg = qp_ref[:, pl.ds((h * 2 + p) * PEER_HALF, PEER_HALF)]
            s = _dot_nt(keys_ref[h * 2 + p], qseg)
            tops.append(_top_rows(s))
        (v1, i1), (v2, i2) = tops
        cand = jnp.concatenate([v1[a:a + 1, :] + v2 for a in range(PEER_TOPK)], axis=0)
        cidx = jnp.concatenate([i1[a:a + 1, :] * PEER_NKEYS + i2 for a in range(PEER_TOPK)], axis=0)
        best, eidx = _top_rows(cand, cidx)
        e = jnp.exp(best - best[0:1, :])
        gate_rows.append(e / jnp.sum(e, axis=0, keepdims=True))
        idx_rows.append(eidx)
    gate = jnp.concatenate(gate_rows, axis=0)
    idx = jnp.concatenate(idx_rows, axis=0)
    gate_ref[...] = gate.T
    idx_ref[...] = idx.T.astype(jnp.int32)


def peer_topk(qp, sub_keys):
    t = qp.shape[0]
    keys = sub_keys.reshape(PEER_HEADS * 2, PEER_NKEYS, PEER_HALF).astype(BF16)
    return pl.pallas_call(
        _peer_topk_kernel,
        grid=(t // PEER_TOK,),
        in_specs=[pl.BlockSpec((PEER_TOK, PEER_HEADS * PEER_QDIM), lambda i: (i, 0)),
                  pl.BlockSpec((PEER_HEADS * 2, PEER_NKEYS, PEER_HALF), lambda i: (0, 0, 0))],
        out_specs=[pl.BlockSpec((PEER_TOK, PEER_HK), lambda i: (i, 0)),
                   pl.BlockSpec((PEER_TOK, PEER_HK), lambda i: (i, 0))],
        out_shape=[jax.ShapeDtypeStruct((t, PEER_HK), jnp.int32),
                   jax.ShapeDtypeStruct((t, PEER_HK), F32)],
        compiler_params=pltpu.CompilerParams(
            dimension_semantics=("parallel",), vmem_limit_bytes=VMEM_LIMIT),
        name="peer_topk",
    )(qp, keys)


SUBLANES = 8
PEXP_NBUF = 16
PEXP_SET = 4
PEXP_KG = PEER_HK // SUBLANES
PEXP_DT = D_MODEL // LANES
PEXP_ROW = 2 * PEXP_DT


def _peer_exp_kernel(idx_ref, nidx_ref, xn_ref, gate_ref, x1_ref, uv_ref, o_ref, *scratch):
    bufs, sem_ref = scratch[:PEXP_NBUF], scratch[PEXP_NBUF]
    sets = PEXP_NBUF // PEXP_SET
    step = pl.program_id(0)
    last_step = pl.num_programs(0) - 1

    def row_copy(src_row, slot, k):
        return pltpu.make_async_copy(uv_ref.at[src_row], bufs[slot].at[k // SUBLANES, :, k % SUBLANES, :],
                                     sem_ref.at[slot])

    def start(ids_ref, slot):
        for k in range(PEER_HK):
            row_copy(ids_ref[slot, k], slot, k).start()

    def wait(slot):
        for k in range(PEER_HK):
            row_copy(0, slot, k).wait()

    def hidden(slot):
        x = xn_ref[pl.ds(slot, 1), :].astype(BF16)
        h = jnp.zeros((1, PEER_HK), F32)
        for c in range(PEXP_DT):
            u_c = bufs[slot][:, c, :, :].reshape(PEER_HK, LANES).astype(BF16)
            h = h + _dot_nt(x[:, c * LANES:(c + 1) * LANES], u_c)
        return (gate_ref[pl.ds(slot, 1), :] * _gelu_exact(h)).astype(BF16)

    def project(slot, w):
        out = [_dot(w, bufs[slot][:, PEXP_DT + c, :, :].reshape(PEER_HK, LANES).astype(BF16))
               for c in range(PEXP_DT)]
        o_ref[pl.ds(slot, 1), :] = x1_ref[pl.ds(slot, 1), :] + jnp.concatenate(out, axis=1)

    @pl.when(step == 0)
    def _():
        for slot in range(PEXP_NBUF - PEXP_SET):
            start(idx_ref, slot)

    for s in range(sets):
        slots = range(s * PEXP_SET, (s + 1) * PEXP_SET)
        for slot in slots:
            wait(slot)
        if s == 0:
            for slot in range((sets - 1) * PEXP_SET, PEXP_NBUF):
                start(idx_ref, slot)
        else:
            for slot in range((s - 1) * PEXP_SET, s * PEXP_SET):
                start(nidx_ref, slot)
        ws = [hidden(slot) for slot in slots]
        for slot, w in zip(slots, ws):
            project(slot, w)

    @pl.when(step == last_step)
    def _():
        for slot in range(PEXP_NBUF - PEXP_SET):
            wait(slot)


def peer_exp(idx, xn, gate, x1, uv):
    t, d = xn.shape
    steps = t // PEXP_NBUF
    tok_block = lambda width: pl.BlockSpec((PEXP_NBUF, width), lambda i: (i, 0))
    return pl.pallas_call(
        _peer_exp_kernel,
        grid=(steps,),
        in_specs=[pl.BlockSpec((PEXP_NBUF, PEER_HK), lambda i: (i, 0), memory_space=pltpu.SMEM),
                  pl.BlockSpec((PEXP_NBUF, PEER_HK), lambda i: (jnp.minimum(i + 1, steps - 1), 0),
                               memory_space=pltpu.SMEM),
                  tok_block(d), tok_block(PEER_HK), tok_block(d),
                  pl.BlockSpec(memory_space=pl.ANY)],
        out_specs=tok_block(d),
        out_shape=jax.ShapeDtypeStruct((t, d), F32),
        scratch_shapes=[pltpu.VMEM((PEXP_KG, PEXP_ROW, SUBLANES, LANES), F32)] * PEXP_NBUF
                       + [pltpu.SemaphoreType.DMA((PEXP_NBUF,))],
        compiler_params=pltpu.CompilerParams(
            dimension_semantics=("arbitrary",), vmem_limit_bytes=VMEM_LIMIT),
        name="peer_exp",
    )(idx, idx, xn, gate, x1, uv)


def kernel(x, norm_mix_g, w_in, w_gk2, b_gk, gla_norm_g, q_norm_g, k_norm_g, attn_sinks, w_branch_a, w_branch_b, w_out, norm_ffn_g, w_peer_q, peer_sub_keys, peer_u, peer_v):
    bsz, seqlen, d = x.shape
    t = bsz * seqlen
    depth = w_in.shape[0]
    x2d = x.reshape(t, d)
    for l in range(depth):
        proj = in_proj(x2d, norm_mix_g[l], _relayout_w_in(w_in[l]))
        w_gk2_pad = jnp.pad(w_gk2[l], ((0, LANES - GLA_GATE_RANK), (0, 0)))
        o_a = gla(proj, w_gk2_pad, b_gk[l], gla_norm_g[l], bsz, seqlen)
        o_b = swa(proj, q_norm_g[l], k_norm_g[l], attn_sinks[l], bsz, seqlen)
        x1, xn, qp = merge(o_a, o_b, proj, x2d, w_branch_a[l], w_branch_b[l], w_out[l],
                           norm_ffn_g[l], w_peer_q[l])
        idx, gate = peer_topk(qp, peer_sub_keys[l])
        uv = jnp.concatenate([peer_u[l], peer_v[l]], axis=1).reshape(-1, PEXP_ROW, LANES)
        x2d = peer_exp(idx, xn, gate, x1, uv)
    return x2d.reshape(bsz, seqlen, d)
```

```python
import functools

import jax
import jax.numpy as jnp
import numpy as np
from jax import lax
from jax.experimental import pallas as pl
from jax.experimental.pallas import tpu as pltpu

D_MODEL = 1024
GLA_HEADS = 4
GLA_DK = 128
GLA_DV = 256
GLA_GATE_RANK = 16
GLA_GATE_NORM = 16.0
GLA_CHUNK = 64
SWA_HEADS = 16
SWA_KV_HEADS = 4
SWA_GROUP = SWA_HEADS // SWA_KV_HEADS
SWA_HEAD_DIM = 64
SWA_WINDOW = 128
SWA_BLOCK = 128
ROPE_THETA = 500000.0
ROPE_DIM = SWA_HEAD_DIM // 4
ROPE_HALF = ROPE_DIM // 2
PEER_HEADS = 8
PEER_NKEYS = 128
PEER_QDIM = 256
PEER_TOPK = 16
PEER_HK = PEER_HEADS * PEER_TOPK
NORM_EPS = 1e-6

GLA_KW = GLA_HEADS * GLA_DK
GLA_VW = GLA_HEADS * GLA_DV
SWA_QW = SWA_HEADS * SWA_HEAD_DIM
SWA_KVW = SWA_KV_HEADS * SWA_HEAD_DIM
IN_SPLITS = (GLA_KW, GLA_KW, GLA_VW, GLA_VW, GLA_GATE_RANK, SWA_QW, SWA_KVW, SWA_KVW, D_MODEL, D_MODEL)
IN_OFFSETS = tuple(int(v) for v in np.cumsum((0,) + IN_SPLITS)[:-1])

LANES = 128
COL_GQ = 0
COL_GK = COL_GQ + GLA_KW
COL_GV = COL_GK + GLA_KW
COL_GR = COL_GV + GLA_VW
COL_SQ = COL_GR + GLA_VW
COL_SK = COL_SQ + SWA_QW
COL_SV = COL_SK + SWA_QW
COL_GA = COL_SV + SWA_QW
COL_GB = COL_GA + D_MODEL
COL_GLR = COL_GB + D_MODEL
PROJ_W = COL_GLR + LANES

VMEM_LIMIT = 48 * 1024 * 1024

F32 = jnp.float32
BF16 = jnp.bfloat16
HIGHEST = lax.Precision.HIGHEST


def _dot(a, b, precision=None):
    return jnp.dot(a, b, preferred_element_type=F32, precision=precision)


def _dot_nt(a, b):
    return lax.dot_general(a, b, (((1,), (1,)), ((), ())), preferred_element_type=F32)


def _gelu_exact(x):
    return 0.5 * x * (1.0 + lax.erf(x * (2.0 ** -0.5)))


def _dot_tn(a, b):
    return lax.dot_general(a, b, (((0,), (0,)), ((), ())), preferred_element_type=F32)


def _in_proj_kernel(x_ref, g_ref, w_ref, o_ref, xn_ref):
    @pl.when(pl.program_id(1) == 0)
    def _():
        x = x_ref[...]
        y = x * lax.rsqrt(jnp.mean(x * x, axis=-1, keepdims=True) + NORM_EPS) * g_ref[...]
        xn_ref[...] = y.astype(BF16)

    o_ref[...] = _dot(xn_ref[...], w_ref[...])


def in_proj(x2d, g, w_bf16, tm=1024, tn=640):
    t, d = x2d.shape
    n = w_bf16.shape[1]
    return pl.pallas_call(
        _in_proj_kernel,
        grid=(t // tm, n // tn),
        in_specs=[pl.BlockSpec((tm, d), lambda i, j: (i, 0)),
                  pl.BlockSpec((1, d), lambda i, j: (0, 0)),
                  pl.BlockSpec((d, tn), lambda i, j: (0, j))],
        out_specs=pl.BlockSpec((tm, tn), lambda i, j: (i, j)),
        out_shape=jax.ShapeDtypeStruct((t, n), F32),
        scratch_shapes=[pltpu.VMEM((tm, d), BF16)],
        compiler_params=pltpu.CompilerParams(
            dimension_semantics=("parallel", "arbitrary"), vmem_limit_bytes=VMEM_LIMIT),
        name="in_proj",
    )(x2d, g.reshape(1, d), w_bf16)


def _relayout_w_in(w_in):
    seg = [w_in[:, o:o + s] for o, s in zip(IN_OFFSETS, IN_SPLITS)]
    gq, gk, gv, gr, glr, sq, sk, sv, ga, gb = seg
    d = w_in.shape[0]

    def rep(w):
        w = w.reshape(d, SWA_KV_HEADS, 1, SWA_HEAD_DIM)
        return jnp.broadcast_to(w, (d, SWA_KV_HEADS, SWA_GROUP, SWA_HEAD_DIM)).reshape(d, SWA_QW)

    glr_pad = jnp.pad(glr, ((0, 0), (0, LANES - GLA_GATE_RANK)))
    return jnp.concatenate([gq, gk, gv, gr, sq, rep(sk), rep(sv), ga, gb, glr_pad], axis=1).astype(BF16)


GLA_STEP = 256


def _gla_kernel(q_ref, k_ref, v_ref, gr_ref, glr_ref, w2_ref, b2_ref, g_ref, o_ref, st_ref):
    @pl.when(pl.program_id(2) == 0)
    def _():
        st_ref[...] = jnp.zeros_like(st_ref)

    c = GLA_CHUNK
    row = lax.broadcasted_iota(jnp.int32, (c, c), 0)
    col = lax.broadcasted_iota(jnp.int32, (c, c), 1)
    causal = col <= row
    tril = causal.astype(F32)
    w2 = w2_ref[...]
    b2 = b2_ref[...]
    gain = g_ref[...]
    for ci in range(GLA_STEP // c):
        sl = pl.ds(ci * c, c)
        q = q_ref[sl, :] * (GLA_DK ** -0.5)
        k = k_ref[sl, :]
        v = v_ref[sl, :].astype(BF16)
        z = _dot(glr_ref[sl, :], w2, HIGHEST) + b2
        log_a = jax.nn.log_sigmoid(z) / GLA_GATE_NORM
        b = _dot(tril, log_a, HIGHEST)
        b_mid = b[c // 2:c // 2 + 1, :]
        b_last = b[c - 1:c, :]
        att = _dot_nt((q * jnp.exp(b - b_mid)).astype(BF16), (k * jnp.exp(b_mid - b)).astype(BF16))
        att = jnp.where(causal, att, 0.0)
        o = _dot(att.astype(BF16), v)
        st = st_ref[...]
        o = o + _dot_nt((q * jnp.exp(b)).astype(BF16), st.astype(BF16))
        k_end = (k * jnp.exp(b_last - b)).astype(BF16)
        st_ref[...] = st * jnp.exp(b_last) + _dot_tn(v, k_end)
        y = o * lax.rsqrt(jnp.mean(o * o, axis=-1, keepdims=True) + NORM_EPS) * gain
        o_ref[sl, :] = (y * jax.nn.silu(gr_ref[sl, :])).astype(o_ref.dtype)


def gla(proj, w_gk2_pad, b_gk, gla_norm_g, bsz, seqlen):
    t = bsz * seqlen
    ns = seqlen // GLA_STEP
    rowmap = lambda b, h, s: b * ns + s
    kq, kk = COL_GQ // GLA_DK, COL_GK // GLA_DK
    kv, kr = COL_GV // GLA_DV, COL_GR // GLA_DV
    kl = COL_GLR // LANES
    return pl.pallas_call(
        _gla_kernel,
        grid=(bsz, GLA_HEADS, ns),
        in_specs=[pl.BlockSpec((GLA_STEP, GLA_DK), lambda b, h, s: (rowmap(b, h, s), kq + h)),
                  pl.BlockSpec((GLA_STEP, GLA_DK), lambda b, h, s: (rowmap(b, h, s), kk + h)),
                  pl.BlockSpec((GLA_STEP, GLA_DV), lambda b, h, s: (rowmap(b, h, s), kv + h)),
                  pl.BlockSpec((GLA_STEP, GLA_DV), lambda b, h, s: (rowmap(b, h, s), kr + h)),
                  pl.BlockSpec((GLA_STEP, LANES), lambda b, h, s: (rowmap(b, h, s), kl)),
                  pl.BlockSpec((LANES, GLA_DK), lambda b, h, s: (0, h)),
                  pl.BlockSpec((1, GLA_DK), lambda b, h, s: (0, h)),
                  pl.BlockSpec((1, GLA_DV), lambda b, h, s: (0, 0))],
        out_specs=pl.BlockSpec((GLA_STEP, GLA_DV), lambda b, h, s: (rowmap(b, h, s), h)),
        out_shape=jax.ShapeDtypeStruct((t, GLA_VW), BF16),
        scratch_shapes=[pltpu.VMEM((GLA_DV, GLA_DK), F32)],
        compiler_params=pltpu.CompilerParams(
            dimension_semantics=("parallel", "parallel", "arbitrary"), vmem_limit_bytes=VMEM_LIMIT),
        name="gla",
    )(proj, proj, proj, proj, proj, w_gk2_pad, b_gk.reshape(1, GLA_KW), gla_norm_g.reshape(1, GLA_DV))


SWA_GW = SWA_GROUP * SWA_HEAD_DIM


def _swa_kernel(sinks_ref, q_ref, k_ref, v_ref, cos_ref, sin_ref, qg_ref, kg_ref, bd_ref, o_ref,
                kprev_ref, vprev_ref):
    n = pl.program_id(1)

    @pl.when(n == 0)
    def _():
        kprev_ref[...] = jnp.zeros_like(kprev_ref)
        vprev_ref[...] = jnp.zeros_like(vprev_ref)

    L = SWA_BLOCK
    cos = cos_ref[...]
    sin = sin_ref[...]
    bd = bd_ref[...]
    lane = lax.broadcasted_iota(jnp.int32, (L, SWA_GW), 1)
    seg = lane // SWA_HEAD_DIM
    first_half = (lane % SWA_HEAD_DIM) < ROPE_HALF

    def norm_rope(x, gain):
        ms = _dot(x * x, bd, HIGHEST) * (1.0 / SWA_HEAD_DIM)
        y = x * lax.rsqrt(ms + NORM_EPS) * gain
        partner = jnp.where(first_half, pltpu.roll(y, SWA_GW - ROPE_HALF, 1), pltpu.roll(y, ROPE_HALF, 1))
        return y * cos + partner * sin

    qi = lax.broadcasted_iota(jnp.int32, (L, L), 0)
    ki = lax.broadcasted_iota(jnp.int32, (L, L), 1)
    mask_cur = ki <= qi
    mask_prev = ki > qi + jnp.where(n > 0, 0, L)
    neg_inf = F32(-jnp.inf)
    for j in range(SWA_KV_HEADS):
        cs = pl.ds(j * SWA_GW, SWA_GW)
        q = (norm_rope(q_ref[:, cs], qg_ref[...]) * (SWA_HEAD_DIM ** -0.5)).astype(BF16)
        k_cur = norm_rope(k_ref[:, cs], kg_ref[...]).astype(BF16)
        v_cur = v_ref[:, cs].astype(BF16)
        k_prev = kprev_ref[:, cs]
        v_prev = vprev_ref[:, cs]
        acc = jnp.zeros((L, SWA_GW), F32)
        for g in range(SWA_GROUP):
            qm = jnp.where(seg == g, q, jnp.zeros_like(q))
            s_cur = jnp.where(mask_cur, _dot_nt(qm, k_cur), neg_inf)
            s_prev = jnp.where(mask_prev, _dot_nt(qm, k_prev), neg_inf)
            sink = sinks_ref[j * SWA_GROUP + g]
            m = jnp.maximum(jnp.maximum(jnp.max(s_cur, axis=-1, keepdims=True),
                                        jnp.max(s_prev, axis=-1, keepdims=True)), sink)
            p_cur = jnp.exp(s_cur - m)
            p_prev = jnp.exp(s_prev - m)
            denom = (jnp.sum(p_cur, axis=-1, keepdims=True) + jnp.sum(p_prev, axis=-1, keepdims=True)
                     + jnp.exp(sink - m))
            og = (_dot(p_cur.astype(BF16), v_cur) + _dot(p_prev.astype(BF16), v_prev)) / denom
            acc = acc + jnp.where(seg == g, og, 0.0)
        o_ref[:, cs] = acc.astype(o_ref.dtype)
        kprev_ref[:, cs] = k_cur
        vprev_ref[:, cs] = v_cur


def _rope_tables(seqlen):
    pos = jnp.arange(seqlen, dtype=F32)
    inv_freq = ROPE_THETA ** (-jnp.arange(0, ROPE_DIM, 2, dtype=F32) / ROPE_DIM)
    ang = pos[:, None] * inv_freq[None, :]
    cos, sin = jnp.cos(ang), jnp.sin(ang)
    rest = SWA_HEAD_DIM - ROPE_DIM
    cos_h = jnp.concatenate([cos, cos, jnp.ones((seqlen, rest), F32)], axis=1)
    sin_h = jnp.concatenate([-sin, sin, jnp.zeros((seqlen, rest), F32)], axis=1)
    return jnp.tile(cos_h, (1, SWA_GROUP)), jnp.tile(sin_h, (1, SWA_GROUP))


def swa(proj, q_norm_g, k_norm_g, sinks, bsz, seqlen):
    t = bsz * seqlen
    nb = seqlen // SWA_BLOCK
    cos_t, sin_t = _rope_tables(seqlen)
    head = np.arange(SWA_GW) // SWA_HEAD_DIM
    blockdiag = jnp.asarray((head[:, None] == head[None, :]).astype(np.float32))
    rowmap = lambda b, n: b * nb + n
    cq, ck, cv = COL_SQ // SWA_QW, COL_SK // SWA_QW, COL_SV // SWA_QW
    grid_spec = pltpu.PrefetchScalarGridSpec(
        num_scalar_prefetch=1,
        grid=(bsz, nb),
        in_specs=[pl.BlockSpec((SWA_BLOCK, SWA_QW), lambda b, n, s: (rowmap(b, n), cq)),
                  pl.BlockSpec((SWA_BLOCK, SWA_QW), lambda b, n, s: (rowmap(b, n), ck)),
                  pl.BlockSpec((SWA_BLOCK, SWA_QW), lambda b, n, s: (rowmap(b, n), cv)),
                  pl.BlockSpec((SWA_BLOCK, SWA_GW), lambda b, n, s: (n, 0)),
                  pl.BlockSpec((SWA_BLOCK, SWA_GW), lambda b, n, s: (n, 0)),
                  pl.BlockSpec((1, SWA_GW), lambda b, n, s: (0, 0)),
                  pl.BlockSpec((1, SWA_GW), lambda b, n, s: (0, 0)),
                  pl.BlockSpec((SWA_GW, SWA_GW), lambda b, n, s: (0, 0))],
        out_specs=pl.BlockSpec((SWA_BLOCK, SWA_QW), lambda b, n, s: (rowmap(b, n), 0)),
        scratch_shapes=[pltpu.VMEM((SWA_BLOCK, SWA_QW), BF16), pltpu.VMEM((SWA_BLOCK, SWA_QW), BF16)],
    )
    return pl.pallas_call(
        _swa_kernel,
        grid_spec=grid_spec,
        out_shape=jax.ShapeDtypeStruct((t, SWA_QW), BF16),
        compiler_params=pltpu.CompilerParams(
            dimension_semantics=("parallel", "arbitrary"), vmem_limit_bytes=VMEM_LIMIT),
        name="swa",
    )(sinks, proj, proj, proj, cos_t, sin_t,
      jnp.tile(q_norm_g, SWA_GROUP).reshape(1, SWA_GW), jnp.tile(k_norm_g, SWA_GROUP).reshape(1, SWA_GW),
      blockdiag)


def _merge_kernel(oa_ref, ob_ref, ga_ref, gb_ref, x_ref, wa_ref, wb_ref, wo_ref, g_ref, wq_ref,
                  x1_ref, xn_ref, qp_ref):
    y_a = _dot(oa_ref[...], wa_ref[...])
    y_b = _dot(ob_ref[...], wb_ref[...])
    merged = jax.nn.sigmoid(ga_ref[...]) * y_a + jax.nn.sigmoid(gb_ref[...]) * y_b
    x1 = x_ref[...] + _dot(merged.astype(BF16), wo_ref[...])
    x1_ref[...] = x1
    xn = x1 * lax.rsqrt(jnp.mean(x1 * x1, axis=-1, keepdims=True) + NORM_EPS) * g_ref[...]
    xn_ref[...] = xn
    qp_ref[...] = _dot(xn.astype(BF16), wq_ref[...]).astype(qp_ref.dtype)


def merge(o_a, o_b, proj, x2d, w_a, w_b, w_o, norm_g, w_pq, tm=256):
    t, d = x2d.shape
    nq = w_pq.shape[1]
    ca, cb = COL_GA // D_MODEL, COL_GB // D_MODEL
    full = lambda shape: pl.BlockSpec(shape, lambda i: (0, 0))
    return pl.pallas_call(
        _merge_kernel,
        grid=(t // tm,),
        in_specs=[pl.BlockSpec((tm, GLA_VW), lambda i: (i, 0)),
                  pl.BlockSpec((tm, SWA_QW), lambda i: (i, 0)),
                  pl.BlockSpec((tm, d), lambda i: (i, ca)),
                  pl.BlockSpec((tm, d), lambda i: (i, cb)),
                  pl.BlockSpec((tm, d), lambda i: (i, 0)),
                  full((GLA_VW, d)), full((SWA_QW, d)), full((d, d)), full((1, d)), full((d, nq))],
        out_specs=[pl.BlockSpec((tm, d), lambda i: (i, 0)),
                   pl.BlockSpec((tm, d), lambda i: (i, 0)),
                   pl.BlockSpec((tm, nq), lambda i: (i, 0))],
        out_shape=[jax.ShapeDtypeStruct((t, d), F32),
                   jax.ShapeDtypeStruct((t, d), F32),
                   jax.ShapeDtypeStruct((t, nq), BF16)],
        compiler_params=pltpu.CompilerParams(
            dimension_semantics=("parallel",), vmem_limit_bytes=VMEM_LIMIT),
        name="merge",
    )(o_a, o_b, proj, proj, x2d, w_a.astype(BF16), w_b.astype(BF16), w_o.astype(BF16),
      norm_g.reshape(1, d), w_pq.astype(BF16))


PEER_TOK = LANES
PEER_HALF = PEER_QDIM // 2
PEER_NCAND = PEER_TOPK * PEER_TOPK


def _top_rows(s, payload=None):
    r = s.shape[0]
    rid = lax.broadcasted_iota(jnp.int32, s.shape, 0).astype(F32)
    vals, rows = [], []
    for _ in range(PEER_TOPK):
        m = jnp.max(s, axis=0, keepdims=True)
        first = jnp.min(jnp.where(s == m, rid, F32(r)), axis=0, keepdims=True)
        hit = rid == first
        vals.append(m)
        if payload is None:
            rows.append(first)
        else:
            rows.append(jnp.max(jnp.where(hit, payload, -1.0), axis=0, keepdims=True))
        s = jnp.where(hit, -jnp.inf, s)
    return jnp.concatenate(vals, axis=0), jnp.concatenate(rows, axis=0)


def _peer_topk_kernel(qp_ref, keys_ref, idx_ref, gate_ref):
    idx_rows, gate_rows = [], []
    for h in range(PEER_HEADS):
        tops = []
        for p in range(2):
            qseg = qp_ref[:, pl.ds((h * 2 + p) * PEER_HALF, PEER_HALF)]
            s = _dot_nt(keys_ref[h * 2 + p], qseg)
            tops.append(_top_rows(s))
        (v1, i1), (v2, i2) = tops
        cand = jnp.concatenate([v1[a:a + 1, :] + v2 for a in range(PEER_TOPK)], axis=0)
        cidx = jnp.concatenate([i1[a:a + 1, :] * PEER_NKEYS + i2 for a in range(PEER_TOPK)], axis=0)
        best, eidx = _top_rows(cand, cidx)
        e = jnp.exp(best - best[0:1, :])
        gate_rows.append(e / jnp.sum(e, axis=0, keepdims=True))
        idx_rows.append(eidx)
    gate = jnp.concatenate(gate_rows, axis=0)
    idx = jnp.concatenate(idx_rows, axis=0)
    gate_ref[...] = gate.T
    idx_ref[...] = idx.T.astype(jnp.int32)


def peer_topk(qp, sub_keys):
    t = qp.shape[0]
    keys = sub_keys.reshape(PEER_HEADS * 2, PEER_NKEYS, PEER_HALF).astype(BF16)
    return pl.pallas_call(
        _peer_topk_kernel,
        grid=(t // PEER_TOK,),
        in_specs=[pl.BlockSpec((PEER_TOK, PEER_HEADS * PEER_QDIM), lambda i: (i, 0)),
                  pl.BlockSpec((PEER_HEADS * 2, PEER_NKEYS, PEER_HALF), lambda i: (0, 0, 0))],
        out_specs=[pl.BlockSpec((PEER_TOK, PEER_HK), lambda i: (i, 0)),
                   pl.BlockSpec((PEER_TOK, PEER_HK), lambda i: (i, 0))],
        out_shape=[jax.ShapeDtypeStruct((t, PEER_HK), jnp.int32),
                   jax.ShapeDtypeStruct((t, PEER_HK), F32)],
        compiler_params=pltpu.CompilerParams(
            dimension_semantics=("parallel",), vmem_limit_bytes=VMEM_LIMIT),
        name="peer_topk",
    )(qp, keys)


SUBLANES = 8
PEXP_NBUF = 16
PEXP_SET = 4
PEXP_DMA_QUEUES = 2
PEXP_KG = PEER_HK // SUBLANES
PEXP_DT = D_MODEL // LANES
PEXP_ROW = 2 * PEXP_DT


def _peer_exp_kernel(idx_ref, nidx_ref, xn_ref, gate_ref, x1_ref, uv_ref, o_ref, *scratch):
    bufs, sem_ref = scratch[:PEXP_NBUF], scratch[PEXP_NBUF]
    sets = PEXP_NBUF // PEXP_SET
    step = pl.program_id(0)
    last_step = pl.num_programs(0) - 1

    def row_copy(src_row, slot, k):
        return pltpu.make_async_copy(uv_ref.at[src_row], bufs[slot].at[k // SUBLANES, :, k % SUBLANES, :],
                                     sem_ref.at[slot])

    def start(ids_ref, slot):
        for k in range(PEER_HK):
            row_copy(ids_ref[slot, k], slot, k).start(priority=k % PEXP_DMA_QUEUES)

    def wait(slot):
        for k in range(PEER_HK):
            row_copy(0, slot, k).wait()

    def hidden(slot):
        x = xn_ref[pl.ds(slot, 1), :].astype(BF16)
        h = jnp.zeros((1, PEER_HK), F32)
        for c in range(PEXP_DT):
            u_c = bufs[slot][:, c, :, :].reshape(PEER_HK, LANES).astype(BF16)
            h = h + _dot_nt(x[:, c * LANES:(c + 1) * LANES], u_c)
        return (gate_ref[pl.ds(slot, 1), :] * _gelu_exact(h)).astype(BF16)

    def project(slot, w):
        out = [_dot(w, bufs[slot][:, PEXP_DT + c, :, :].reshape(PEER_HK, LANES).astype(BF16))
               for c in range(PEXP_DT)]
        o_ref[pl.ds(slot, 1), :] = x1_ref[pl.ds(slot, 1), :] + jnp.concatenate(out, axis=1)

    @pl.when(step == 0)
    def _():
        for slot in range(PEXP_NBUF - PEXP_SET):
            start(idx_ref, slot)

    for s in range(sets):
        slots = range(s * PEXP_SET, (s + 1) * PEXP_SET)
        for slot in slots:
            wait(slot)
        if s == 0:
            for slot in range((sets - 1) * PEXP_SET, PEXP_NBUF):
                start(idx_ref, slot)
        else:
            for slot in range((s - 1) * PEXP_SET, s * PEXP_SET):
                start(nidx_ref, slot)
        ws = [hidden(slot) for slot in slots]
        for slot, w in zip(slots, ws):
            project(slot, w)

    @pl.when(step == last_step)
    def _():
        for slot in range(PEXP_NBUF - PEXP_SET):
            wait(slot)


def peer_exp(idx, xn, gate, x1, uv):
    t, d = xn.shape
    steps = t // PEXP_NBUF
    tok_block = lambda width: pl.BlockSpec((PEXP_NBUF, width), lambda i: (i, 0))
    return pl.pallas_call(
        _peer_exp_kernel,
        grid=(steps,),
        in_specs=[pl.BlockSpec((PEXP_NBUF, PEER_HK), lambda i: (i, 0), memory_space=pltpu.SMEM),
                  pl.BlockSpec((PEXP_NBUF, PEER_HK), lambda i: (jnp.minimum(i + 1, steps - 1), 0),
                               memory_space=pltpu.SMEM),
                  tok_block(d), tok_block(PEER_HK), tok_block(d),
                  pl.BlockSpec(memory_space=pl.ANY)],
        out_specs=tok_block(d),
        out_shape=jax.ShapeDtypeStruct((t, d), F32),
        scratch_shapes=[pltpu.VMEM((PEXP_KG, PEXP_ROW, SUBLANES, LANES), F32)] * PEXP_NBUF
                       + [pltpu.SemaphoreType.DMA((PEXP_NBUF,))],
        compiler_params=pltpu.CompilerParams(
            dimension_semantics=("arbitrary",), vmem_limit_bytes=VMEM_LIMIT),
        name="peer_exp",
    )(idx, idx, xn, gate, x1, uv)


def kernel(x, norm_mix_g, w_in, w_gk2, b_gk, gla_norm_g, q_norm_g, k_norm_g, attn_sinks, w_branch_a, w_branch_b, w_out, norm_ffn_g, w_peer_q, peer_sub_keys, peer_u, peer_v):
    bsz, seqlen, d = x.shape
    t = bsz * seqlen
    depth = w_in.shape[0]
    x2d = x.reshape(t, d)
    for l in range(depth):
        proj = in_proj(x2d, norm_mix_g[l], _relayout_w_in(w_in[l]))
        w_gk2_pad = jnp.pad(w_gk2[l], ((0, LANES - GLA_GATE_RANK), (0, 0)))
        o_a = gla(proj, w_gk2_pad, b_gk[l], gla_norm_g[l], bsz, seqlen)
        o_b = swa(proj, q_norm_g[l], k_norm_g[l], attn_sinks[l], bsz, seqlen)
        x1, xn, qp = merge(o_a, o_b, proj, x2d, w_branch_a[l], w_branch_b[l], w_out[l],
                           norm_ffn_g[l], w_peer_q[l])
        idx, gate = peer_topk(qp, peer_sub_keys[l])
        uv = jnp.concatenate([peer_u[l], peer_v[l]], axis=1).reshape(-1, PEXP_ROW, LANES)
        x2d = peer_exp(idx, xn, gate, x1, uv)
    return x2d.reshape(bsz, seqlen, d)
```

```python
import functools

import jax
import jax.numpy as jnp
import numpy as np
from jax import lax
from jax.experimental import pallas as pl
from jax.experimental.pallas import tpu as pltpu

D_MODEL = 1024
GLA_HEADS = 4
GLA_DK = 128
GLA_DV = 256
GLA_GATE_RANK = 16
GLA_GATE_NORM = 16.0
GLA_CHUNK = 64
SWA_HEADS = 16
SWA_KV_HEADS = 4
SWA_GROUP = SWA_HEADS // SWA_KV_HEADS
SWA_HEAD_DIM = 64
SWA_WINDOW = 128
SWA_BLOCK = 128
ROPE_THETA = 500000.0
ROPE_DIM = SWA_HEAD_DIM // 4
ROPE_HALF = ROPE_DIM // 2
PEER_HEADS = 8
PEER_NKEYS = 128
PEER_QDIM = 256
PEER_TOPK = 16
PEER_HK = PEER_HEADS * PEER_TOPK
NORM_EPS = 1e-6

GLA_KW = GLA_HEADS * GLA_DK
GLA_VW = GLA_HEADS * GLA_DV
SWA_QW = SWA_HEADS * SWA_HEAD_DIM
SWA_KVW = SWA_KV_HEADS * SWA_HEAD_DIM
IN_SPLITS = (GLA_KW, GLA_KW, GLA_VW, GLA_VW, GLA_GATE_RANK, SWA_QW, SWA_KVW, SWA_KVW, D_MODEL, D_MODEL)
IN_OFFSETS = tuple(int(v) for v in np.cumsum((0,) + IN_SPLITS)[:-1])

LANES = 128
SUBLANES = 8
COL_GQ = 0
COL_GK = COL_GQ + GLA_KW
COL_GV = COL_GK + GLA_KW
COL_GR = COL_GV + GLA_VW
COL_SQ = COL_GR + GLA_VW
COL_SK = COL_SQ + SWA_QW
COL_SV = COL_SK + SWA_QW
COL_GA = COL_SV + SWA_QW
COL_GB = COL_GA + D_MODEL
COL_GLR = COL_GB + D_MODEL
PROJ_W = COL_GLR + LANES

VMEM_LIMIT = 48 * 1024 * 1024

F32 = jnp.float32
BF16 = jnp.bfloat16
HIGHEST = lax.Precision.HIGHEST


def _dot(a, b, precision=None):
    return jnp.dot(a, b, preferred_element_type=F32, precision=precision)


def _dot_nt(a, b):
    return lax.dot_general(a, b, (((1,), (1,)), ((), ())), preferred_element_type=F32)


def _gelu_exact(x):
    return 0.5 * x * (1.0 + lax.erf(x * (2.0 ** -0.5)))


def _dot_tn(a, b):
    return lax.dot_general(a, b, (((0,), (0,)), ((), ())), preferred_element_type=F32)


def _in_proj_kernel(x_ref, g_ref, w_ref, o_ref, xn_ref):
    @pl.when(pl.program_id(1) == 0)
    def _():
        x = x_ref[...]
        y = x * lax.rsqrt(jnp.mean(x * x, axis=-1, keepdims=True) + NORM_EPS) * g_ref[...]
        xn_ref[...] = y.astype(BF16)

    o_ref[...] = _dot(xn_ref[...], w_ref[...])


def in_proj(x2d, g, w_bf16, tm=1024, tn=1664):
    t, d = x2d.shape
    n = w_bf16.shape[1]
    return pl.pallas_call(
        _in_proj_kernel,
        grid=(t // tm, n // tn),
        in_specs=[pl.BlockSpec((tm, d), lambda i, j: (i, 0)),
                  pl.BlockSpec((1, d), lambda i, j: (0, 0)),
                  pl.BlockSpec((d, tn), lambda i, j: (0, j))],
        out_specs=pl.BlockSpec((tm, tn), lambda i, j: (i, j)),
        out_shape=jax.ShapeDtypeStruct((t, n), F32),
        scratch_shapes=[pltpu.VMEM((tm, d), BF16)],
        compiler_params=pltpu.CompilerParams(
            dimension_semantics=("parallel", "arbitrary"), vmem_limit_bytes=VMEM_LIMIT),
        name="in_proj",
    )(x2d, g.reshape(1, d), w_bf16)


def _relayout_w_in(w_in):
    seg = [w_in[:, o:o + s] for o, s in zip(IN_OFFSETS, IN_SPLITS)]
    gq, gk, gv, gr, glr, sq, sk, sv, ga, gb = seg
    d = w_in.shape[0]

    def rep(w):
        w = w.reshape(d, SWA_KV_HEADS, 1, SWA_HEAD_DIM)
        return jnp.broadcast_to(w, (d, SWA_KV_HEADS, SWA_GROUP, SWA_HEAD_DIM)).reshape(d, SWA_QW)

    glr_pad = jnp.pad(glr, ((0, 0), (0, LANES - GLA_GATE_RANK)))
    return jnp.concatenate([gq, gk, gv, gr, sq, rep(sk), rep(sv), ga, gb, glr_pad], axis=1).astype(BF16)


GLA_STEP = 256
GLA_HPS = 4


def _gla_kernel(q_ref, k_ref, v_ref, gr_ref, glr_ref, w2_ref, b2_ref, g_ref, cum_ref, o_ref, st_ref):
    @pl.when(pl.program_id(2) == 0)
    def _():
        st_ref[...] = jnp.zeros_like(st_ref)

    c, n = GLA_CHUNK, GLA_STEP
    row = lax.broadcasted_iota(jnp.int32, (c, c), 0)
    col = lax.broadcasted_iota(jnp.int32, (c, c), 1)
    causal = col <= row
    gain = g_ref[...]
    tril = cum_ref[...]
    glr = glr_ref[...]
    pre = []
    for hh in range(GLA_HPS):
        ks = slice(hh * GLA_DK, (hh + 1) * GLA_DK)
        q = q_ref[:, ks] * (GLA_DK ** -0.5)
        k = k_ref[:, ks]
        z = _dot(glr, w2_ref[:, ks], HIGHEST) + b2_ref[:, ks]
        log_a = jax.nn.log_sigmoid(z) / GLA_GATE_NORM
        parts = []
        for ci in range(n // c):
            bc = _dot(tril, log_a[ci * c:(ci + 1) * c], HIGHEST)
            parts.append((bc, jnp.broadcast_to(bc[c // 2:c // 2 + 1], bc.shape),
                          jnp.broadcast_to(bc[c - 1:c], bc.shape)))
        b, b_mid, b_last = (jnp.concatenate([p[i] for p in parts], axis=0) for i in range(3))
        pre.append(dict(q_in=(q * jnp.exp(b - b_mid)).astype(BF16), k_in=(k * jnp.exp(b_mid - b)).astype(BF16),
                        q_st=(q * jnp.exp(b)).astype(BF16), k_end=(k * jnp.exp(b_last - b)).astype(BF16),
                        decay=jnp.exp(b_last)))
    states = [st_ref[hh] for hh in range(GLA_HPS)]
    for ci in range(n // c):
        sl = slice(ci * c, (ci + 1) * c)
        for hh in range(GLA_HPS):
            p, st = pre[hh], states[hh]
            vs = pl.ds(hh * GLA_DV, GLA_DV)
            v = v_ref[pl.ds(ci * c, c), vs].astype(BF16)
            att = jnp.where(causal, _dot_nt(p["q_in"][sl], p["k_in"][sl]), 0.0)
            o = _dot(att.astype(BF16), v) + _dot_nt(p["q_st"][sl], st.astype(BF16))
            states[hh] = st * p["decay"][ci * c:ci * c + 1, :] + _dot_tn(v, p["k_end"][sl])
            y = o * lax.rsqrt(jnp.mean(o * o, axis=-1, keepdims=True) + NORM_EPS) * gain
            o_ref[pl.ds(ci * c, c), vs] = (y * jax.nn.silu(gr_ref[pl.ds(ci * c, c), vs])).astype(o_ref.dtype)
    for hh in range(GLA_HPS):
        st_ref[hh] = states[hh]


def _gla_cum_matrix():
    return jnp.asarray(np.tril(np.ones((GLA_CHUNK, GLA_CHUNK), np.float32)))


def gla(proj, w_gk2_pad, b_gk, gla_norm_g, bsz, seqlen):
    t = bsz * seqlen
    ns = seqlen // GLA_STEP
    rowmap = lambda b, h, s: b * ns + s
    kw, vw = GLA_HPS * GLA_DK, GLA_HPS * GLA_DV
    kq, kk = COL_GQ // kw, COL_GK // kw
    kv, kr = COL_GV // vw, COL_GR // vw
    kl = COL_GLR // LANES
    return pl.pallas_call(
        _gla_kernel,
        grid=(bsz, GLA_HEADS // GLA_HPS, ns),
        in_specs=[pl.BlockSpec((GLA_STEP, kw), lambda b, h, s: (rowmap(b, h, s), kq + h)),
                  pl.BlockSpec((GLA_STEP, kw), lambda b, h, s: (rowmap(b, h, s), kk + h)),
                  pl.BlockSpec((GLA_STEP, vw), lambda b, h, s: (rowmap(b, h, s), kv + h)),
                  pl.BlockSpec((GLA_STEP, vw), lambda b, h, s: (rowmap(b, h, s), kr + h)),
                  pl.BlockSpec((GLA_STEP, LANES), lambda b, h, s: (rowmap(b, h, s), kl)),
                  pl.BlockSpec((LANES, kw), lambda b, h, s: (0, h)),
                  pl.BlockSpec((1, kw), lambda b, h, s: (0, h)),
                  pl.BlockSpec((1, GLA_DV), lambda b, h, s: (0, 0)),
                  pl.BlockSpec((GLA_CHUNK, GLA_CHUNK), lambda b, h, s: (0, 0))],
        out_specs=pl.BlockSpec((GLA_STEP, vw), lambda b, h, s: (rowmap(b, h, s), h)),
        out_shape=jax.ShapeDtypeStruct((t, GLA_VW), BF16),
        scratch_shapes=[pltpu.VMEM((GLA_HPS, GLA_DV, GLA_DK), F32)],
        compiler_params=pltpu.CompilerParams(
            dimension_semantics=("parallel", "parallel", "arbitrary"), vmem_limit_bytes=VMEM_LIMIT),
        name="gla",
    )(proj, proj, proj, proj, proj, w_gk2_pad, b_gk.reshape(1, GLA_KW), gla_norm_g.reshape(1, GLA_DV),
      _gla_cum_matrix())


SWA_GW = SWA_GROUP * SWA_HEAD_DIM


def _swa_kernel(sinks_ref, q_ref, k_ref, v_ref, cos_ref, sin_ref, qg_ref, kg_ref, bd_ref, o_ref,
                kprev_ref, vprev_ref):
    n = pl.program_id(1)

    @pl.when(n == 0)
    def _():
        kprev_ref[...] = jnp.zeros_like(kprev_ref)
        vprev_ref[...] = jnp.zeros_like(vprev_ref)

    L = SWA_BLOCK
    cos = cos_ref[...]
    sin = sin_ref[...]
    bd = bd_ref[...]
    lane = lax.broadcasted_iota(jnp.int32, (L, SWA_GW), 1)
    seg = lane // SWA_HEAD_DIM
    first_half = (lane % SWA_HEAD_DIM) < ROPE_HALF

    def norm_rope(x, gain):
        x2 = x * x
        hi = x2.astype(BF16)
        lo = (x2 - hi.astype(F32)).astype(BF16)
        ms = (_dot(hi, bd) + _dot(lo, bd)) * (1.0 / SWA_HEAD_DIM)
        y = x * lax.rsqrt(ms + NORM_EPS) * gain
        partner = jnp.where(first_half, pltpu.roll(y, SWA_GW - ROPE_HALF, 1), pltpu.roll(y, ROPE_HALF, 1))
        return y * cos + partner * sin

    rows = SWA_GROUP * L
    qi = lax.broadcasted_iota(jnp.int32, (rows, L), 0) % L
    ki = lax.broadcasted_iota(jnp.int32, (rows, L), 1)
    row_head = lax.broadcasted_iota(jnp.int32, (rows, 1), 0) // L
    mask_cur = ki <= qi
    mask_prev = ki > qi + jnp.where(n > 0, 0, L)
    neg_inf = F32(-jnp.inf)
    for j in range(SWA_KV_HEADS):
        cs = pl.ds(j * SWA_GW, SWA_GW)
        q = (norm_rope(q_ref[:, cs], qg_ref[...]) * (SWA_HEAD_DIM ** -0.5)).astype(BF16)
        k_cur = norm_rope(k_ref[:, cs], kg_ref[...]).astype(BF16)
        v_cur = v_ref[:, cs].astype(BF16)
        k_prev = kprev_ref[:, cs]
        v_prev = vprev_ref[:, cs]
        qs = jnp.concatenate([jnp.where(seg == g, q, jnp.zeros_like(q)) for g in range(SWA_GROUP)], axis=0)
        sink = jnp.zeros((rows, 1), F32)
        for g in range(SWA_GROUP):
            sink = jnp.where(row_head == g, sinks_ref[j * SWA_GROUP + g], sink)
        s_cur = jnp.where(mask_cur, _dot_nt(qs, k_cur), neg_inf)
        s_prev = jnp.where(mask_prev, _dot_nt(qs, k_prev), neg_inf)
        m = jnp.maximum(jnp.maximum(jnp.max(s_cur, axis=-1, keepdims=True),
                                    jnp.max(s_prev, axis=-1, keepdims=True)), sink)
        p_cur = jnp.exp(s_cur - m)
        p_prev = jnp.exp(s_prev - m)
        denom = (jnp.sum(p_cur, axis=-1, keepdims=True) + jnp.sum(p_prev, axis=-1, keepdims=True)
                 + jnp.exp(sink - m))
        og = (_dot(p_cur.astype(BF16), v_cur) + _dot(p_prev.astype(BF16), v_prev)) / denom
        acc = jnp.zeros((L, SWA_GW), F32)
        for g in range(SWA_GROUP):
            acc = acc + jnp.where(seg == g, og[g * L:(g + 1) * L], 0.0)
        o_ref[:, cs] = acc.astype(o_ref.dtype)
        kprev_ref[:, cs] = k_cur
        vprev_ref[:, cs] = v_cur


def _rope_tables(seqlen):
    pos = jnp.arange(seqlen, dtype=F32)
    inv_freq = ROPE_THETA ** (-jnp.arange(0, ROPE_DIM, 2, dtype=F32) / ROPE_DIM)
    ang = pos[:, None] * inv_freq[None, :]
    cos, sin = jnp.cos(ang), jnp.sin(ang)
    rest = SWA_HEAD_DIM - ROPE_DIM
    cos_h = jnp.concatenate([cos, cos, jnp.ones((seqlen, rest), F32)], axis=1)
    sin_h = jnp.concatenate([-sin, sin, jnp.zeros((seqlen, rest), F32)], axis=1)
    return jnp.tile(cos_h, (1, SWA_GROUP)), jnp.tile(sin_h, (1, SWA_GROUP))


def swa(proj, q_norm_g, k_norm_g, sinks, bsz, seqlen):
    t = bsz * seqlen
    nb = seqlen // SWA_BLOCK
    cos_t, sin_t = _rope_tables(seqlen)
    head = np.arange(SWA_GW) // SWA_HEAD_DIM
    blockdiag = jnp.asarray((head[:, None] == head[None, :]).astype(np.float32), dtype=BF16)
    rowmap = lambda b, n: b * nb + n
    cq, ck, cv = COL_SQ // SWA_QW, COL_SK // SWA_QW, COL_SV // SWA_QW
    grid_spec = pltpu.PrefetchScalarGridSpec(
        num_scalar_prefetch=1,
        grid=(bsz, nb),
        in_specs=[pl.BlockSpec((SWA_BLOCK, SWA_QW), lambda b, n, s: (rowmap(b, n), cq)),
                  pl.BlockSpec((SWA_BLOCK, SWA_QW), lambda b, n, s: (rowmap(b, n), ck)),
                  pl.BlockSpec((SWA_BLOCK, SWA_QW), lambda b, n, s: (rowmap(b, n), cv)),
                  pl.BlockSpec((SWA_BLOCK, SWA_GW), lambda b, n, s: (n, 0)),
                  pl.BlockSpec((SWA_BLOCK, SWA_GW), lambda b, n, s: (n, 0)),
                  pl.BlockSpec((1, SWA_GW), lambda b, n, s: (0, 0)),
                  pl.BlockSpec((1, SWA_GW), lambda b, n, s: (0, 0)),
                  pl.BlockSpec((SWA_GW, SWA_GW), lambda b, n, s: (0, 0))],
        out_specs=pl.BlockSpec((SWA_BLOCK, SWA_QW), lambda b, n, s: (rowmap(b, n), 0)),
        scratch_shapes=[pltpu.VMEM((SWA_BLOCK, SWA_QW), BF16), pltpu.VMEM((SWA_BLOCK, SWA_QW), BF16)],
    )
    return pl.pallas_call(
        _swa_kernel,
        grid_spec=grid_spec,
        out_shape=jax.ShapeDtypeStruct((t, SWA_QW), BF16),
        compiler_params=pltpu.CompilerParams(
            dimension_semantics=("parallel", "arbitrary"), vmem_limit_bytes=VMEM_LIMIT),
        name="swa",
    )(sinks, proj, proj, proj, cos_t, sin_t,
      jnp.tile(q_norm_g, SWA_GROUP).reshape(1, SWA_GW), jnp.tile(k_norm_g, SWA_GROUP).reshape(1, SWA_GW),
      blockdiag)


def _merge_kernel(oa_ref, ob_ref, ga_ref, gb_ref, x_ref, wa_ref, wb_ref, wo_ref, g_ref, wq_ref,
                  x1_ref, xn_ref, qp_ref):
    y_a = _dot(oa_ref[...], wa_ref[...])
    y_b = _dot(ob_ref[...], wb_ref[...])
    merged = jax.nn.sigmoid(ga_ref[...]) * y_a + jax.nn.sigmoid(gb_ref[...]) * y_b
    x1 = x_ref[...] + _dot(merged.astype(BF16), wo_ref[...])
    x1_ref[...] = x1
    xn = x1 * lax.rsqrt(jnp.mean(x1 * x1, axis=-1, keepdims=True) + NORM_EPS) * g_ref[...]
    xn_ref[...] = xn
    qp_ref[...] = _dot(xn.astype(BF16), wq_ref[...]).astype(qp_ref.dtype)


def merge(o_a, o_b, proj, x2d, w_a, w_b, w_o, norm_g, w_pq, tm=256):
    t, d = x2d.shape
    nq = w_pq.shape[1]
    ca, cb = COL_GA // D_MODEL, COL_GB // D_MODEL
    full = lambda shape: pl.BlockSpec(shape, lambda i: (0, 0))
    return pl.pallas_call(
        _merge_kernel,
        grid=(t // tm,),
        in_specs=[pl.BlockSpec((tm, GLA_VW), lambda i: (i, 0)),
                  pl.BlockSpec((tm, SWA_QW), lambda i: (i, 0)),
                  pl.BlockSpec((tm, d), lambda i: (i, ca)),
                  pl.BlockSpec((tm, d), lambda i: (i, cb)),
                  pl.BlockSpec((tm, d), lambda i: (i, 0)),
                  full((GLA_VW, d)), full((SWA_QW, d)), full((d, d)), full((1, d)), full((d, nq))],
        out_specs=[pl.BlockSpec((tm, d), lambda i: (i, 0)),
                   pl.BlockSpec((tm, d), lambda i: (i, 0)),
                   pl.BlockSpec((tm, nq), lambda i: (i, 0))],
        out_shape=[jax.ShapeDtypeStruct((t, d), F32),
                   jax.ShapeDtypeStruct((t, d), F32),
                   jax.ShapeDtypeStruct((t, nq), BF16)],
        compiler_params=pltpu.CompilerParams(
            dimension_semantics=("parallel",), vmem_limit_bytes=VMEM_LIMIT),
        name="merge",
    )(o_a, o_b, proj, proj, x2d, w_a.astype(BF16), w_b.astype(BF16), w_o.astype(BF16),
      norm_g.reshape(1, d), w_pq.astype(BF16))


PEER_TOK = LANES
PEER_HALF = PEER_QDIM // 2
PEER_NCAND = PEER_TOPK * PEER_TOPK


def _top_rows(s, pos, payload=None):
    vals, rows = [], []
    for _ in range(PEER_TOPK):
        m = jnp.max(s, axis=0, keepdims=True)
        first = jnp.min(jnp.where(s == m, pos, F32(2 ** 24)), axis=0, keepdims=True)
        hit = pos == first
        vals.append(m)
        if payload is None:
            rows.append(first)
        else:
            rows.append(jnp.max(jnp.where(hit, payload, -1.0), axis=0, keepdims=True))
        s = jnp.where(hit, -jnp.inf, s)
    return jnp.concatenate(vals, axis=0), jnp.concatenate(rows, axis=0)


def _candidate_pieces():
    pieces = []
    single = []
    for a in range(PEER_TOPK):
        nb = PEER_TOPK // (a + 1)
        if nb >= 2:
            for b0 in range(0, nb, SUBLANES):
                pieces.append((a, 1, b0, SUBLANES))
        else:
            single.append(a)
    assert len(single) % SUBLANES == 0 and single == list(range(single[0], PEER_TOPK))
    for a0 in range(single[0], PEER_TOPK, SUBLANES):
        pieces.append((a0, SUBLANES, 0, 1))
    return pieces


def _route_subkeys(q_half, keys_half):
    key_id = lax.broadcasted_iota(jnp.int32, (PEER_NKEYS, PEER_TOK), 0).astype(F32)
    return _top_rows(_dot_nt(keys_half, q_half), key_id)


def _route_candidates(top_lo, top_hi):
    (v1, i1), (v2, i2) = top_lo, top_hi
    sub = lax.broadcasted_iota(jnp.int32, (SUBLANES, PEER_TOK), 0).astype(F32)
    pieces = _candidate_pieces()
    cpos = jnp.concatenate([(a0 + (sub if na > 1 else 0.0)) * PEER_TOPK + (b0 + (sub if nb > 1 else 0.0))
                            for a0, na, b0, nb in pieces], axis=0)
    cand = jnp.concatenate([v1[a0:a0 + na, :] + v2[b0:b0 + nb, :] for a0, na, b0, nb in pieces], axis=0)
    cidx = jnp.concatenate([i1[a0:a0 + na, :] * PEER_NKEYS + i2[b0:b0 + nb, :]
                            for a0, na, b0, nb in pieces], axis=0)
    return _top_rows(cand, cpos, cidx)


def _route_gates(best):
    e = jnp.exp(best - best[0:1, :])
    return e / jnp.sum(e, axis=0, keepdims=True)


def _route_head(q_lo, q_hi, keys_lo, keys_hi):
    best, eidx = _route_candidates(_route_subkeys(q_lo, keys_lo), _route_subkeys(q_hi, keys_hi))
    return eidx, _route_gates(best)


def _peer_topk_kernel(qp_ref, keys_ref, idx_ref, gate_ref):
    idx_rows, gate_rows = [], []
    for h in range(PEER_HEADS):
        eidx, gate_h = _route_head(qp_ref[:, pl.ds(2 * h * PEER_HALF, PEER_HALF)],
                                   qp_ref[:, pl.ds((2 * h + 1) * PEER_HALF, PEER_HALF)],
                                   keys_ref[2 * h], keys_ref[2 * h + 1])
        gate_rows.append(gate_h)
        idx_rows.append(eidx)
    gate = jnp.concatenate(gate_rows, axis=0)
    idx = jnp.concatenate(idx_rows, axis=0)
    gate_ref[...] = gate.T
    idx_ref[...] = idx.T.astype(jnp.int32)


def peer_topk(qp, sub_keys):
    t = qp.shape[0]
    keys = sub_keys.reshape(PEER_HEADS * 2, PEER_NKEYS, PEER_HALF).astype(BF16)
    return pl.pallas_call(
        _peer_topk_kernel,
        grid=(t // PEER_TOK,),
        in_specs=[pl.BlockSpec((PEER_TOK, PEER_HEADS * PEER_QDIM), lambda i: (i, 0)),
                  pl.BlockSpec((PEER_HEADS * 2, PEER_NKEYS, PEER_HALF), lambda i: (0, 0, 0))],
        out_specs=[pl.BlockSpec((PEER_TOK, PEER_HK), lambda i: (i, 0)),
                   pl.BlockSpec((PEER_TOK, PEER_HK), lambda i: (i, 0))],
        out_shape=[jax.ShapeDtypeStruct((t, PEER_HK), jnp.int32),
                   jax.ShapeDtypeStruct((t, PEER_HK), F32)],
        compiler_params=pltpu.CompilerParams(
            dimension_semantics=("arbitrary",), vmem_limit_bytes=VMEM_LIMIT),
        name="peer_topk",
    )(qp, keys)


PEXP_NBUF = 16
PEXP_SET = 4
PEXP_DMA_QUEUES = 2
PEXP_DT = D_MODEL // LANES
PEXP_ROW = 2 * PEXP_DT
PEXP_PITCH = PEXP_ROW + 1


def _peer_exp_kernel(idx_ref, nidx_ref, xn_ref, gate_ref, x1_ref, uv_ref, o_ref, *scratch):
    bufs, sem_ref = scratch[:PEXP_NBUF], scratch[PEXP_NBUF]
    sets = PEXP_NBUF // PEXP_SET
    step = pl.program_id(0)
    last_step = pl.num_programs(0) - 1

    def row_copy(src_row, slot, k):
        return pltpu.make_async_copy(uv_ref.at[src_row], bufs[slot].at[pl.ds(k * PEXP_PITCH, PEXP_ROW)],
                                     sem_ref.at[slot])

    def start(ids_ref, slot):
        for k in range(PEER_HK):
            row_copy(ids_ref[slot, k], slot, k).start(priority=k % PEXP_DMA_QUEUES)

    def wait(slot):
        for k in range(PEER_HK):
            row_copy(0, slot, k).wait()

    def hidden(slot):
        x = xn_ref[pl.ds(slot, 1), :].astype(BF16)
        h = jnp.zeros((1, PEER_HK), F32)
        for c in range(PEXP_DT):
            u_c = bufs[slot][pl.ds(c, PEER_HK, stride=PEXP_PITCH), :].astype(BF16)
            h = h + _dot_nt(x[:, c * LANES:(c + 1) * LANES], u_c)
        return (gate_ref[pl.ds(slot, 1), :] * _gelu_exact(h)).astype(BF16)

    def project(slot, w):
        out = [_dot(w, bufs[slot][pl.ds(PEXP_DT + c, PEER_HK, stride=PEXP_PITCH), :].astype(BF16))
               for c in range(PEXP_DT)]
        o_ref[pl.ds(slot, 1), :] = x1_ref[pl.ds(slot, 1), :] + jnp.concatenate(out, axis=1)

    @pl.when(step == 0)
    def _():
        for slot in range(PEXP_NBUF - PEXP_SET):
            start(idx_ref, slot)

    for s in range(sets):
        slots = range(s * PEXP_SET, (s + 1) * PEXP_SET)
        for slot in slots:
            wait(slot)
        if s == 0:
            for slot in range((sets - 1) * PEXP_SET, PEXP_NBUF):
                start(idx_ref, slot)
        else:
            for slot in range((s - 1) * PEXP_SET, s * PEXP_SET):
                start(nidx_ref, slot)
        ws = [hidden(slot) for slot in slots]
        for slot, w in zip(slots, ws):
            project(slot, w)

    @pl.when(step == last_step)
    def _():
        for slot in range(PEXP_NBUF - PEXP_SET):
            wait(slot)


def peer_exp(idx, xn, gate, x1, uv):
    t, d = xn.shape
    steps = t // PEXP_NBUF
    tok_block = lambda width: pl.BlockSpec((PEXP_NBUF, width), lambda i: (i, 0))
    return pl.pallas_call(
        _peer_exp_kernel,
        grid=(steps,),
        in_specs=[pl.BlockSpec((PEXP_NBUF, PEER_HK), lambda i: (i, 0), memory_space=pltpu.SMEM),
                  pl.BlockSpec((PEXP_NBUF, PEER_HK), lambda i: (jnp.minimum(i + 1, steps - 1), 0),
                               memory_space=pltpu.SMEM),
                  tok_block(d), tok_block(PEER_HK), tok_block(d),
                  pl.BlockSpec(memory_space=pl.ANY)],
        out_specs=tok_block(d),
        out_shape=jax.ShapeDtypeStruct((t, d), F32),
        scratch_shapes=[pltpu.VMEM((PEER_HK * PEXP_PITCH, LANES), F32)] * PEXP_NBUF
                       + [pltpu.SemaphoreType.DMA((PEXP_NBUF,))],
        compiler_params=pltpu.CompilerParams(
            dimension_semantics=("arbitrary",), vmem_limit_bytes=VMEM_LIMIT),
        name="peer_exp",
    )(idx, idx, xn, gate, x1, uv)


def kernel(x, norm_mix_g, w_in, w_gk2, b_gk, gla_norm_g, q_norm_g, k_norm_g, attn_sinks, w_branch_a, w_branch_b, w_out, norm_ffn_g, w_peer_q, peer_sub_keys, peer_u, peer_v):
    bsz, seqlen, d = x.shape
    t = bsz * seqlen
    depth = w_in.shape[0]
    x2d = x.reshape(t, d)
    for l in range(depth):
        proj = in_proj(x2d, norm_mix_g[l], _relayout_w_in(w_in[l]))
        w_gk2_pad = jnp.pad(w_gk2[l], ((0, LANES - GLA_GATE_RANK), (0, 0)))
        o_a = gla(proj, w_gk2_pad, b_gk[l], gla_norm_g[l], bsz, seqlen)
        o_b = swa(proj, q_norm_g[l], k_norm_g[l], attn_sinks[l], bsz, seqlen)
        x1, xn, qp = merge(o_a, o_b, proj, x2d, w_branch_a[l], w_branch_b[l], w_out[l],
                           norm_ffn_g[l], w_peer_q[l])
        idx, gate = peer_topk(qp, peer_sub_keys[l])
        uv = jnp.concatenate([peer_u[l], peer_v[l]], axis=1).reshape(-1, PEXP_ROW, LANES)
        x2d = peer_exp(idx, xn, gate, x1, uv)
    return x2d.reshape(bsz, seqlen, d)
```

```python
import functools

import jax
import jax.numpy as jnp
import numpy as np
from jax import lax
from jax.experimental import pallas as pl
from jax.experimental.pallas import tpu as pltpu

D_MODEL = 1024
GLA_HEADS = 4
GLA_DK = 128
GLA_DV = 256
GLA_GATE_RANK = 16
GLA_GATE_NORM = 16.0
GLA_CHUNK = 64
SWA_HEADS = 16
SWA_KV_HEADS = 4
SWA_GROUP = SWA_HEADS // SWA_KV_HEADS
SWA_HEAD_DIM = 64
SWA_WINDOW = 128
SWA_BLOCK = 128
ROPE_THETA = 500000.0
ROPE_DIM = SWA_HEAD_DIM // 4
ROPE_HALF = ROPE_DIM // 2
PEER_HEADS = 8
PEER_NKEYS = 128
PEER_QDIM = 256
PEER_TOPK = 16
PEER_HK = PEER_HEADS * PEER_TOPK
NORM_EPS = 1e-6

GLA_KW = GLA_HEADS * GLA_DK
GLA_VW = GLA_HEADS * GLA_DV
SWA_QW = SWA_HEADS * SWA_HEAD_DIM
SWA_KVW = SWA_KV_HEADS * SWA_HEAD_DIM
IN_SPLITS = (GLA_KW, GLA_KW, GLA_VW, GLA_VW, GLA_GATE_RANK, SWA_QW, SWA_KVW, SWA_KVW, D_MODEL, D_MODEL)
IN_OFFSETS = tuple(int(v) for v in np.cumsum((0,) + IN_SPLITS)[:-1])

LANES = 128
SUBLANES = 8
COL_GQ = 0
COL_GK = COL_GQ + GLA_KW
COL_GV = COL_GK + GLA_KW
COL_GR = COL_GV + GLA_VW
COL_SQ = COL_GR + GLA_VW
COL_GA = COL_SQ + SWA_QW
COL_GB = COL_GA + D_MODEL
COL_SK = COL_GB + D_MODEL
COL_SV = COL_SK + SWA_KVW
COL_GLR = COL_SV + SWA_KVW
GLR_PAD = 2 * LANES
PROJ_W = COL_GLR + GLR_PAD

VMEM_LIMIT = 48 * 1024 * 1024

F32 = jnp.float32
BF16 = jnp.bfloat16
HIGHEST = lax.Precision.HIGHEST


def _dot(a, b, precision=None):
    return jnp.dot(a, b, preferred_element_type=F32, precision=precision)


def _dot_nt(a, b):
    return lax.dot_general(a, b, (((1,), (1,)), ((), ())), preferred_element_type=F32)


def _gelu_exact(x):
    return 0.5 * x * (1.0 + lax.erf(x * (2.0 ** -0.5)))


def _dot_tn(a, b):
    return lax.dot_general(a, b, (((0,), (0,)), ((), ())), preferred_element_type=F32)


def _in_proj_kernel(x_ref, g_ref, w_ref, o_ref, xn_ref):
    @pl.when(pl.program_id(1) == 0)
    def _():
        x = x_ref[...]
        y = x * lax.rsqrt(jnp.mean(x * x, axis=-1, keepdims=True) + NORM_EPS) * g_ref[...]
        xn_ref[...] = y.astype(BF16)

    o_ref[...] = _dot(xn_ref[...], w_ref[...])


def in_proj(x2d, g, w_bf16, tm=1024, tn=2304):
    t, d = x2d.shape
    n = w_bf16.shape[1]
    return pl.pallas_call(
        _in_proj_kernel,
        grid=(t // tm, n // tn),
        in_specs=[pl.BlockSpec((tm, d), lambda i, j: (i, 0)),
                  pl.BlockSpec((1, d), lambda i, j: (0, 0)),
                  pl.BlockSpec((d, tn), lambda i, j: (0, j))],
        out_specs=pl.BlockSpec((tm, tn), lambda i, j: (i, j)),
        out_shape=jax.ShapeDtypeStruct((t, n), F32),
        scratch_shapes=[pltpu.VMEM((tm, d), BF16)],
        compiler_params=pltpu.CompilerParams(
            dimension_semantics=("parallel", "arbitrary"), vmem_limit_bytes=VMEM_LIMIT),
        name="in_proj",
    )(x2d, g.reshape(1, d), w_bf16)


def _relayout_w_in(w_in):
    seg = [w_in[:, o:o + s] for o, s in zip(IN_OFFSETS, IN_SPLITS)]
    gq, gk, gv, gr, glr, sq, sk, sv, ga, gb = seg
    glr_pad = jnp.pad(glr, ((0, 0), (0, GLR_PAD - GLA_GATE_RANK)))
    return jnp.concatenate([gq, gk, gv, gr, sq, ga, gb, sk, sv, glr_pad], axis=1).astype(BF16)


GLA_STEP = 256
GLA_HPS = 4


def _gla_kernel(q_ref, k_ref, v_ref, gr_ref, glr_ref, w2_ref, b2_ref, g_ref, cum_ref, o_ref, st_ref):
    @pl.when(pl.program_id(2) == 0)
    def _():
        st_ref[...] = jnp.zeros_like(st_ref)

    c, n = GLA_CHUNK, GLA_STEP
    row = lax.broadcasted_iota(jnp.int32, (c, c), 0)
    col = lax.broadcasted_iota(jnp.int32, (c, c), 1)
    causal = col <= row
    gain = g_ref[...]
    tril = cum_ref[...]
    glr = glr_ref[...]
    pre = []
    for hh in range(GLA_HPS):
        ks = slice(hh * GLA_DK, (hh + 1) * GLA_DK)
        q = q_ref[:, ks] * (GLA_DK ** -0.5)
        k = k_ref[:, ks]
        z = _dot(glr, w2_ref[:, ks], HIGHEST) + b2_ref[:, ks]
        log_a = jax.nn.log_sigmoid(z) / GLA_GATE_NORM
        parts = []
        for ci in range(n // c):
            bc = _dot(tril, log_a[ci * c:(ci + 1) * c], HIGHEST)
            parts.append((bc, jnp.broadcast_to(bc[c // 2:c // 2 + 1], bc.shape),
                          jnp.broadcast_to(bc[c - 1:c], bc.shape)))
        b, b_mid, b_last = (jnp.concatenate([p[i] for p in parts], axis=0) for i in range(3))
        pre.append(dict(q_in=(q * jnp.exp(b - b_mid)).astype(BF16), k_in=(k * jnp.exp(b_mid - b)).astype(BF16),
                        q_st=(q * jnp.exp(b)).astype(BF16), k_end=(k * jnp.exp(b_last - b)).astype(BF16),
                        decay=jnp.exp(b_last)))
    states = [st_ref[hh] for hh in range(GLA_HPS)]
    for ci in range(n // c):
        sl = slice(ci * c, (ci + 1) * c)
        for hh in range(GLA_HPS):
            p, st = pre[hh], states[hh]
            vs = pl.ds(hh * GLA_DV, GLA_DV)
            v = v_ref[pl.ds(ci * c, c), vs].astype(BF16)
            att = jnp.where(causal, _dot_nt(p["q_in"][sl], p["k_in"][sl]), 0.0)
            o = _dot(att.astype(BF16), v) + _dot_nt(p["q_st"][sl], st.astype(BF16))
            states[hh] = st * p["decay"][ci * c:ci * c + 1, :] + _dot_tn(v, p["k_end"][sl])
            y = o * lax.rsqrt(jnp.mean(o * o, axis=-1, keepdims=True) + NORM_EPS) * gain
            o_ref[pl.ds(ci * c, c), vs] = (y * jax.nn.silu(gr_ref[pl.ds(ci * c, c), vs])).astype(o_ref.dtype)
    for hh in range(GLA_HPS):
        st_ref[hh] = states[hh]


def _gla_cum_matrix():
    return jnp.asarray(np.tril(np.ones((GLA_CHUNK, GLA_CHUNK), np.float32)))


def gla(proj, w_gk2_pad, b_gk, gla_norm_g, bsz, seqlen):
    t = bsz * seqlen
    ns = seqlen // GLA_STEP
    rowmap = lambda b, h, s: b * ns + s
    kw, vw = GLA_HPS * GLA_DK, GLA_HPS * GLA_DV
    kq, kk = COL_GQ // kw, COL_GK // kw
    kv, kr = COL_GV // vw, COL_GR // vw
    kl = COL_GLR // LANES
    return pl.pallas_call(
        _gla_kernel,
        grid=(bsz, GLA_HEADS // GLA_HPS, ns),
        in_specs=[pl.BlockSpec((GLA_STEP, kw), lambda b, h, s: (rowmap(b, h, s), kq + h)),
                  pl.BlockSpec((GLA_STEP, kw), lambda b, h, s: (rowmap(b, h, s), kk + h)),
                  pl.BlockSpec((GLA_STEP, vw), lambda b, h, s: (rowmap(b, h, s), kv + h)),
                  pl.BlockSpec((GLA_STEP, vw), lambda b, h, s: (rowmap(b, h, s), kr + h)),
                  pl.BlockSpec((GLA_STEP, LANES), lambda b, h, s: (rowmap(b, h, s), kl)),
                  pl.BlockSpec((LANES, kw), lambda b, h, s: (0, h)),
                  pl.BlockSpec((1, kw), lambda b, h, s: (0, h)),
                  pl.BlockSpec((1, GLA_DV), lambda b, h, s: (0, 0)),
                  pl.BlockSpec((GLA_CHUNK, GLA_CHUNK), lambda b, h, s: (0, 0))],
        out_specs=pl.BlockSpec((GLA_STEP, vw), lambda b, h, s: (rowmap(b, h, s), h)),
        out_shape=jax.ShapeDtypeStruct((t, GLA_VW), BF16),
        scratch_shapes=[pltpu.VMEM((GLA_HPS, GLA_DV, GLA_DK), F32)],
        compiler_params=pltpu.CompilerParams(
            dimension_semantics=("parallel", "parallel", "arbitrary"), vmem_limit_bytes=VMEM_LIMIT),
        name="gla",
    )(proj, proj, proj, proj, proj, w_gk2_pad, b_gk.reshape(1, GLA_KW), gla_norm_g.reshape(1, GLA_DV),
      _gla_cum_matrix())


SWA_GW = SWA_GROUP * SWA_HEAD_DIM
assert SWA_KVW == SWA_GW


def _swa_kernel(sinks_ref, q_ref, k_ref, v_ref, cos_ref, sin_ref, qg_ref, kg_ref, bd_ref, rep_ref, o_ref,
                kprev_ref, vprev_ref):
    n = pl.program_id(1)

    @pl.when(n == 0)
    def _():
        kprev_ref[...] = jnp.zeros_like(kprev_ref)
        vprev_ref[...] = jnp.zeros_like(vprev_ref)

    L = SWA_BLOCK
    cos = cos_ref[...]
    sin = sin_ref[...]
    bd = bd_ref[...]
    lane = lax.broadcasted_iota(jnp.int32, (L, SWA_GW), 1)
    seg = lane // SWA_HEAD_DIM
    first_half = (lane % SWA_HEAD_DIM) < ROPE_HALF

    def norm_rope(x, gain):
        x2 = x * x
        hi = x2.astype(BF16)
        lo = (x2 - hi.astype(F32)).astype(BF16)
        ms = (_dot(hi, bd) + _dot(lo, bd)) * (1.0 / SWA_HEAD_DIM)
        y = x * lax.rsqrt(ms + NORM_EPS) * gain
        partner = jnp.where(first_half, pltpu.roll(y, SWA_GW - ROPE_HALF, 1), pltpu.roll(y, ROPE_HALF, 1))
        return y * cos + partner * sin

    rows = SWA_GROUP * L
    qi = lax.broadcasted_iota(jnp.int32, (rows, L), 0) % L
    ki = lax.broadcasted_iota(jnp.int32, (rows, L), 1)
    row_head = lax.broadcasted_iota(jnp.int32, (rows, 1), 0) // L
    mask_cur = ki <= qi
    mask_prev = ki > qi + jnp.where(n > 0, 0, L)
    neg_inf = F32(-jnp.inf)
    k_all = norm_rope(k_ref[...], kg_ref[...]).astype(BF16)
    v_all = v_ref[...].astype(BF16)
    for j in range(SWA_KV_HEADS):
        cs = pl.ds(j * SWA_GW, SWA_GW)
        q = (norm_rope(q_ref[:, cs], qg_ref[...]) * (SWA_HEAD_DIM ** -0.5)).astype(BF16)
        k_cur = _dot(k_all, rep_ref[j]).astype(BF16)
        v_cur = _dot(v_all, rep_ref[j]).astype(BF16)
        k_prev = kprev_ref[:, cs]
        v_prev = vprev_ref[:, cs]
        qs = jnp.concatenate([jnp.where(seg == g, q, jnp.zeros_like(q)) for g in range(SWA_GROUP)], axis=0)
        sink = jnp.zeros((rows, 1), F32)
        for g in range(SWA_GROUP):
            sink = jnp.where(row_head == g, sinks_ref[j * SWA_GROUP + g], sink)
        s_cur = jnp.where(mask_cur, _dot_nt(qs, k_cur), neg_inf)
        s_prev = jnp.where(mask_prev, _dot_nt(qs, k_prev), neg_inf)
        m = jnp.maximum(jnp.maximum(jnp.max(s_cur, axis=-1, keepdims=True),
                                    jnp.max(s_prev, axis=-1, keepdims=True)), sink)
        p_cur = jnp.exp(s_cur - m)
        p_prev = jnp.exp(s_prev - m)
        denom = (jnp.sum(p_cur, axis=-1, keepdims=True) + jnp.sum(p_prev, axis=-1, keepdims=True)
                 + jnp.exp(sink - m))
        og = (_dot(p_cur.astype(BF16), v_cur) + _dot(p_prev.astype(BF16), v_prev)) / denom
        acc = jnp.zeros((L, SWA_GW), F32)
        for g in range(SWA_GROUP):
            acc = acc + jnp.where(seg == g, og[g * L:(g + 1) * L], 0.0)
        o_ref[:, cs] = acc.astype(o_ref.dtype)
        kprev_ref[:, cs] = k_cur
        vprev_ref[:, cs] = v_cur


def _rope_tables(seqlen):
    pos = jnp.arange(seqlen, dtype=F32)
    inv_freq = ROPE_THETA ** (-jnp.arange(0, ROPE_DIM, 2, dtype=F32) / ROPE_DIM)
    ang = pos[:, None] * inv_freq[None, :]
    cos, sin = jnp.cos(ang), jnp.sin(ang)
    rest = SWA_HEAD_DIM - ROPE_DIM
    cos_h = jnp.concatenate([cos, cos, jnp.ones((seqlen, rest), F32)], axis=1)
    sin_h = jnp.concatenate([-sin, sin, jnp.zeros((seqlen, rest), F32)], axis=1)
    return jnp.tile(cos_h, (1, SWA_GROUP)), jnp.tile(sin_h, (1, SWA_GROUP))


def swa(proj, q_norm_g, k_norm_g, sinks, bsz, seqlen):
    t = bsz * seqlen
    nb = seqlen // SWA_BLOCK
    cos_t, sin_t = _rope_tables(seqlen)
    head = np.arange(SWA_GW) // SWA_HEAD_DIM
    blockdiag = jnp.asarray((head[:, None] == head[None, :]).astype(np.float32), dtype=BF16)
    src = np.arange(SWA_KVW)
    dst = np.arange(SWA_GW)
    rep = np.stack([(src[:, None] // SWA_HEAD_DIM == j) & (src[:, None] % SWA_HEAD_DIM == dst[None, :] % SWA_HEAD_DIM)
                    for j in range(SWA_KV_HEADS)]).astype(np.float32)
    rowmap = lambda b, n: b * nb + n
    cq, ck, cv = COL_SQ // SWA_QW, COL_SK // SWA_KVW, COL_SV // SWA_KVW
    grid_spec = pltpu.PrefetchScalarGridSpec(
        num_scalar_prefetch=1,
        grid=(bsz, nb),
        in_specs=[pl.BlockSpec((SWA_BLOCK, SWA_QW), lambda b, n, s: (rowmap(b, n), cq)),
                  pl.BlockSpec((SWA_BLOCK, SWA_KVW), lambda b, n, s: (rowmap(b, n), ck)),
                  pl.BlockSpec((SWA_BLOCK, SWA_KVW), lambda b, n, s: (rowmap(b, n), cv)),
                  pl.BlockSpec((SWA_BLOCK, SWA_GW), lambda b, n, s: (n, 0)),
                  pl.BlockSpec((SWA_BLOCK, SWA_GW), lambda b, n, s: (n, 0)),
                  pl.BlockSpec((1, SWA_GW), lambda b, n, s: (0, 0)),
                  pl.BlockSpec((1, SWA_GW), lambda b, n, s: (0, 0)),
                  pl.BlockSpec((SWA_GW, SWA_GW), lambda b, n, s: (0, 0)),
                  pl.BlockSpec((SWA_KV_HEADS, SWA_KVW, SWA_GW), lambda b, n, s: (0, 0, 0))],
        out_specs=pl.BlockSpec((SWA_BLOCK, SWA_QW), lambda b, n, s: (rowmap(b, n), 0)),
        scratch_shapes=[pltpu.VMEM((SWA_BLOCK, SWA_QW), BF16), pltpu.VMEM((SWA_BLOCK, SWA_QW), BF16)],
    )
    return pl.pallas_call(
        _swa_kernel,
        grid_spec=grid_spec,
        out_shape=jax.ShapeDtypeStruct((t, SWA_QW), BF16),
        compiler_params=pltpu.CompilerParams(
            dimension_semantics=("parallel", "arbitrary"), vmem_limit_bytes=VMEM_LIMIT),
        name="swa",
    )(sinks, proj, proj, proj, cos_t, sin_t,
      jnp.tile(q_norm_g, SWA_GROUP).reshape(1, SWA_GW), jnp.tile(k_norm_g, SWA_GROUP).reshape(1, SWA_GW),
      blockdiag, jnp.asarray(rep, dtype=BF16))


def _merge_kernel(oa_ref, ob_ref, ga_ref, gb_ref, x_ref, wa_ref, wb_ref, wo_ref, g_ref, wq_ref,
                  x1_ref, xn_ref, qp_ref):
    y_a = _dot(oa_ref[...], wa_ref[...])
    y_b = _dot(ob_ref[...], wb_ref[...])
    merged = jax.nn.sigmoid(ga_ref[...]) * y_a + jax.nn.sigmoid(gb_ref[...]) * y_b
    x1 = x_ref[...] + _dot(merged.astype(BF16), wo_ref[...])
    x1_ref[...] = x1
    xn = x1 * lax.rsqrt(jnp.mean(x1 * x1, axis=-1, keepdims=True) + NORM_EPS) * g_ref[...]
    xn_ref[...] = xn
    qp_ref[...] = _dot(xn.astype(BF16), wq_ref[...]).astype(qp_ref.dtype)


def merge(o_a, o_b, proj, x2d, w_a, w_b, w_o, norm_g, w_pq, tm=256):
    t, d = x2d.shape
    nq = w_pq.shape[1]
    ca, cb = COL_GA // D_MODEL, COL_GB // D_MODEL
    full = lambda shape: pl.BlockSpec(shape, lambda i: (0, 0))
    return pl.pallas_call(
        _merge_kernel,
        grid=(t // tm,),
        in_specs=[pl.BlockSpec((tm, GLA_VW), lambda i: (i, 0)),
                  pl.BlockSpec((tm, SWA_QW), lambda i: (i, 0)),
                  pl.BlockSpec((tm, d), lambda i: (i, ca)),
                  pl.BlockSpec((tm, d), lambda i: (i, cb)),
                  pl.BlockSpec((tm, d), lambda i: (i, 0)),
                  full((GLA_VW, d)), full((SWA_QW, d)), full((d, d)), full((1, d)), full((d, nq))],
        out_specs=[pl.BlockSpec((tm, d), lambda i: (i, 0)),
                   pl.BlockSpec((tm, d), lambda i: (i, 0)),
                   pl.BlockSpec((tm, nq), lambda i: (i, 0))],
        out_shape=[jax.ShapeDtypeStruct((t, d), F32),
                   jax.ShapeDtypeStruct((t, d), F32),
                   jax.ShapeDtypeStruct((t, nq), BF16)],
        compiler_params=pltpu.CompilerParams(
            dimension_semantics=("parallel",), vmem_limit_bytes=VMEM_LIMIT),
        name="merge",
    )(o_a, o_b, proj, proj, x2d, w_a.astype(BF16), w_b.astype(BF16), w_o.astype(BF16),
      norm_g.reshape(1, d), w_pq.astype(BF16))


PEER_TOK = LANES
PEER_HALF = PEER_QDIM // 2
PEER_NCAND = PEER_TOPK * PEER_TOPK


def _top_rows(s, pos, payload=None):
    vals, rows = [], []
    for _ in range(PEER_TOPK):
        m = jnp.max(s, axis=0, keepdims=True)
        first = jnp.min(jnp.where(s == m, pos, F32(2 ** 24)), axis=0, keepdims=True)
        hit = pos == first
        vals.append(m)
        if payload is None:
            rows.append(first)
        else:
            rows.append(jnp.max(jnp.where(hit, payload, -1.0), axis=0, keepdims=True))
        s = jnp.where(hit, -jnp.inf, s)
    return jnp.concatenate(vals, axis=0), jnp.concatenate(rows, axis=0)


def _candidate_pieces():
    pieces = []
    single = []
    for a in range(PEER_TOPK):
        nb = PEER_TOPK // (a + 1)
        if nb >= 2:
            for b0 in range(0, nb, SUBLANES):
                pieces.append((a, 1, b0, SUBLANES))
        else:
            single.append(a)
    assert len(single) % SUBLANES == 0 and single == list(range(single[0], PEER_TOPK))
    for a0 in range(single[0], PEER_TOPK, SUBLANES):
        pieces.append((a0, SUBLANES, 0, 1))
    return pieces


def _route_subkeys(q_half, keys_half):
    key_id = lax.broadcasted_iota(jnp.int32, (PEER_NKEYS, PEER_TOK), 0).astype(F32)
    return _top_rows(_dot_nt(keys_half, q_half), key_id)


def _route_candidates(top_lo, top_hi):
    (v1, i1), (v2, i2) = top_lo, top_hi
    sub = lax.broadcasted_iota(jnp.int32, (SUBLANES, PEER_TOK), 0).astype(F32)
    pieces = _candidate_pieces()
    cpos = jnp.concatenate([(a0 + (sub if na > 1 else 0.0)) * PEER_TOPK + (b0 + (sub if nb > 1 else 0.0))
                            for a0, na, b0, nb in pieces], axis=0)
    cand = jnp.concatenate([v1[a0:a0 + na, :] + v2[b0:b0 + nb, :] for a0, na, b0, nb in pieces], axis=0)
    cidx = jnp.concatenate([i1[a0:a0 + na, :] * PEER_NKEYS + i2[b0:b0 + nb, :]
                            for a0, na, b0, nb in pieces], axis=0)
    return _top_rows(cand, cpos, cidx)


def _route_gates(best):
    e = jnp.exp(best - best[0:1, :])
    return e / jnp.sum(e, axis=0, keepdims=True)


def _route_head(q_lo, q_hi, keys_lo, keys_hi):
    best, eidx = _route_candidates(_route_subkeys(q_lo, keys_lo), _route_subkeys(q_hi, keys_hi))
    return eidx, _route_gates(best)


def _peer_topk_kernel(qp_ref, keys_ref, idx_ref, gate_ref):
    idx_rows, gate_rows = [], []
    for h in range(PEER_HEADS):
        eidx, gate_h = _route_head(qp_ref[:, pl.ds(2 * h * PEER_HALF, PEER_HALF)],
                                   qp_ref[:, pl.ds((2 * h + 1) * PEER_HALF, PEER_HALF)],
                                   keys_ref[2 * h], keys_ref[2 * h + 1])
        gate_rows.append(gate_h)
        idx_rows.append(eidx)
    gate = jnp.concatenate(gate_rows, axis=0)
    idx = jnp.concatenate(idx_rows, axis=0)
    gate_ref[...] = gate.T
    idx_ref[...] = idx.T.astype(jnp.int32)


def peer_topk(qp, sub_keys):
    t = qp.shape[0]
    keys = sub_keys.reshape(PEER_HEADS * 2, PEER_NKEYS, PEER_HALF).astype(BF16)
    return pl.pallas_call(
        _peer_topk_kernel,
        grid=(t // PEER_TOK,),
        in_specs=[pl.BlockSpec((PEER_TOK, PEER_HEADS * PEER_QDIM), lambda i: (i, 0)),
                  pl.BlockSpec((PEER_HEADS * 2, PEER_NKEYS, PEER_HALF), lambda i: (0, 0, 0))],
        out_specs=[pl.BlockSpec((PEER_TOK, PEER_HK), lambda i: (i, 0)),
                   pl.BlockSpec((PEER_TOK, PEER_HK), lambda i: (i, 0))],
        out_shape=[jax.ShapeDtypeStruct((t, PEER_HK), jnp.int32),
                   jax.ShapeDtypeStruct((t, PEER_HK), F32)],
        compiler_params=pltpu.CompilerParams(
            dimension_semantics=("arbitrary",), vmem_limit_bytes=VMEM_LIMIT),
        name="peer_topk",
    )(qp, keys)


PEXP_NBUF = 16
PEXP_SET = 4
PEXP_DMA_QUEUES = 2
PEXP_DT = D_MODEL // LANES
PEXP_ROW = 2 * PEXP_DT
PEXP_PITCH = PEXP_ROW + 1


def _peer_exp_kernel(idx_ref, nidx_ref, xn_ref, gate_ref, x1_ref, uv_ref, o_ref, *scratch):
    bufs, sem_ref = scratch[:PEXP_NBUF], scratch[PEXP_NBUF]
    sets = PEXP_NBUF // PEXP_SET
    step = pl.program_id(0)
    last_step = pl.num_programs(0) - 1

    def row_copy(src_row, slot, k):
        return pltpu.make_async_copy(uv_ref.at[src_row], bufs[slot].at[pl.ds(k * PEXP_PITCH, PEXP_ROW)],
                                     sem_ref.at[slot])

    def start(ids_ref, slot):
        for k in range(PEER_HK):
            row_copy(ids_ref[slot, k], slot, k).start(priority=k % PEXP_DMA_QUEUES)

    def wait(slot):
        for k in range(PEER_HK):
            row_copy(0, slot, k).wait()

    def hidden(slot):
        x = xn_ref[pl.ds(slot, 1), :].astype(BF16)
        h = jnp.zeros((1, PEER_HK), F32)
        for c in range(PEXP_DT):
            u_c = bufs[slot][pl.ds(c, PEER_HK, stride=PEXP_PITCH), :].astype(BF16)
            h = h + _dot_nt(x[:, c * LANES:(c + 1) * LANES], u_c)
        return (gate_ref[pl.ds(slot, 1), :] * _gelu_exact(h)).astype(BF16)

    def project(slot, w):
        out = [_dot(w, bufs[slot][pl.ds(PEXP_DT + c, PEER_HK, stride=PEXP_PITCH), :].astype(BF16))
               for c in range(PEXP_DT)]
        o_ref[pl.ds(slot, 1), :] = x1_ref[pl.ds(slot, 1), :] + jnp.concatenate(out, axis=1)

    @pl.when(step == 0)
    def _():
        for slot in range(PEXP_NBUF - PEXP_SET):
            start(idx_ref, slot)

    for s in range(sets):
        slots = range(s * PEXP_SET, (s + 1) * PEXP_SET)
        for slot in slots:
            wait(slot)
        if s == 0:
            for slot in range((sets - 1) * PEXP_SET, PEXP_NBUF):
                start(idx_ref, slot)
        else:
            for slot in range((s - 1) * PEXP_SET, s * PEXP_SET):
                start(nidx_ref, slot)
        ws = [hidden(slot) for slot in slots]
        for slot, w in zip(slots, ws):
            project(slot, w)

    @pl.when(step == last_step)
    def _():
        for slot in range(PEXP_NBUF - PEXP_SET):
            wait(slot)


def peer_exp(idx, xn, gate, x1, uv):
    t, d = xn.shape
    steps = t // PEXP_NBUF
    tok_block = lambda width: pl.BlockSpec((PEXP_NBUF, width), lambda i: (i, 0))
    return pl.pallas_call(
        _peer_exp_kernel,
        grid=(steps,),
        in_specs=[pl.BlockSpec((PEXP_NBUF, PEER_HK), lambda i: (i, 0), memory_space=pltpu.SMEM),
                  pl.BlockSpec((PEXP_NBUF, PEER_HK), lambda i: (jnp.minimum(i + 1, steps - 1), 0),
                               memory_space=pltpu.SMEM),
                  tok_block(d), tok_block(PEER_HK), tok_block(d),
                  pl.BlockSpec(memory_space=pl.ANY)],
        out_specs=tok_block(d),
        out_shape=jax.ShapeDtypeStruct((t, d), F32),
        scratch_shapes=[pltpu.VMEM((PEER_HK * PEXP_PITCH, LANES), F32)] * PEXP_NBUF
                       + [pltpu.SemaphoreType.DMA((PEXP_NBUF,))],
        compiler_params=pltpu.CompilerParams(
            dimension_semantics=("arbitrary",), vmem_limit_bytes=VMEM_LIMIT),
        name="peer_exp",
    )(idx, idx, xn, gate, x1, uv)


def kernel(x, norm_mix_g, w_in, w_gk2, b_gk, gla_norm_g, q_norm_g, k_norm_g, attn_sinks, w_branch_a, w_branch_b, w_out, norm_ffn_g, w_peer_q, peer_sub_keys, peer_u, peer_v):
    bsz, seqlen, d = x.shape
    t = bsz * seqlen
    depth = w_in.shape[0]
    x2d = x.reshape(t, d)
    for l in range(depth):
        proj = in_proj(x2d, norm_mix_g[l], _relayout_w_in(w_in[l]))
        w_gk2_pad = jnp.pad(w_gk2[l], ((0, LANES - GLA_GATE_RANK), (0, 0)))
        o_a = gla(proj, w_gk2_pad, b_gk[l], gla_norm_g[l], bsz, seqlen)
        o_b = swa(proj, q_norm_g[l], k_norm_g[l], attn_sinks[l], bsz, seqlen)
        x1, xn, qp = merge(o_a, o_b, proj, x2d, w_branch_a[l], w_branch_b[l], w_out[l],
                           norm_ffn_g[l], w_peer_q[l])
        idx, gate = peer_topk(qp, peer_sub_keys[l])
        uv = jnp.concatenate([peer_u[l], peer_v[l]], axis=1).reshape(-1, PEXP_ROW, LANES)
        x2d = peer_exp(idx, xn, gate, x1, uv)
    return x2d.reshape(bsz, seqlen, d)
```

```python
import jax
import jax.numpy as jnp
import numpy as np
from jax import lax
from jax.experimental import pallas as pl
from jax.experimental.pallas import tpu as pltpu

D_MODEL = 1024
GLA_HEADS = 4
GLA_DK = 128
GLA_DV = 256
GLA_GATE_RANK = 16
GLA_GATE_NORM = 16.0
GLA_CHUNK = 64
SWA_HEADS = 16
SWA_KV_HEADS = 4
SWA_GROUP = SWA_HEADS // SWA_KV_HEADS
SWA_HEAD_DIM = 64
SWA_WINDOW = 128
SWA_BLOCK = 128
assert SWA_WINDOW == SWA_BLOCK
ROPE_THETA = 500000.0
ROPE_DIM = SWA_HEAD_DIM // 4
ROPE_HALF = ROPE_DIM // 2
PEER_HEADS = 8
PEER_NKEYS = 128
PEER_QDIM = 256
PEER_TOPK = 16
PEER_HK = PEER_HEADS * PEER_TOPK
NORM_EPS = 1e-6

GLA_KW = GLA_HEADS * GLA_DK
GLA_VW = GLA_HEADS * GLA_DV
SWA_QW = SWA_HEADS * SWA_HEAD_DIM
SWA_KVW = SWA_KV_HEADS * SWA_HEAD_DIM
IN_SPLITS = (GLA_KW, GLA_KW, GLA_VW, GLA_VW, GLA_GATE_RANK, SWA_QW, SWA_KVW, SWA_KVW, D_MODEL, D_MODEL)
IN_OFFSETS = tuple(int(v) for v in np.cumsum((0,) + IN_SPLITS)[:-1])

LANES = 128
SUBLANES = 8
COL_GQ = 0
COL_GK = COL_GQ + GLA_KW
COL_GV = COL_GK + GLA_KW
COL_GR = COL_GV + GLA_VW
COL_SQ = COL_GR + GLA_VW
COL_GA = COL_SQ + SWA_QW
COL_GB = COL_GA + D_MODEL
COL_SK = COL_GB + D_MODEL
COL_SV = COL_SK + SWA_KVW
COL_GLR = COL_SV + SWA_KVW
GLR_PAD = 2 * LANES
PROJ_W = COL_GLR + GLR_PAD

VMEM_LIMIT = 48 * 1024 * 1024

F32 = jnp.float32
BF16 = jnp.bfloat16
HIGHEST = lax.Precision.HIGHEST


def _dot(a, b, precision=None):
    return jnp.dot(a, b, preferred_element_type=F32, precision=precision)


def _dot_nt(a, b):
    return lax.dot_general(a, b, (((1,), (1,)), ((), ())), preferred_element_type=F32)


def _gelu_exact(x):
    return 0.5 * x * (1.0 + lax.erf(x * (2.0 ** -0.5)))


def _dot_tn(a, b):
    return lax.dot_general(a, b, (((0,), (0,)), ((), ())), preferred_element_type=F32)


def _in_proj_kernel(x_ref, g_ref, w_ref, o_ref, xn_ref):
    @pl.when(pl.program_id(1) == 0)
    def _():
        x = x_ref[...]
        y = x * lax.rsqrt(jnp.mean(x * x, axis=-1, keepdims=True) + NORM_EPS) * g_ref[...]
        xn_ref[...] = y.astype(BF16)

    o_ref[...] = _dot(xn_ref[...], w_ref[...])


def in_proj(x2d, g, w_bf16, tm=1024, tn=2304):
    t, d = x2d.shape
    n = w_bf16.shape[1]
    return pl.pallas_call(
        _in_proj_kernel,
        grid=(t // tm, n // tn),
        in_specs=[pl.BlockSpec((tm, d), lambda i, j: (i, 0)),
                  pl.BlockSpec((1, d), lambda i, j: (0, 0)),
                  pl.BlockSpec((d, tn), lambda i, j: (0, j))],
        out_specs=pl.BlockSpec((tm, tn), lambda i, j: (i, j)),
        out_shape=jax.ShapeDtypeStruct((t, n), F32),
        scratch_shapes=[pltpu.VMEM((tm, d), BF16)],
        compiler_params=pltpu.CompilerParams(
            dimension_semantics=("parallel", "arbitrary"), vmem_limit_bytes=VMEM_LIMIT),
        name="in_proj",
    )(x2d, g.reshape(1, d), w_bf16)


def _relayout_w_in(w_in):
    seg = [w_in[:, o:o + s] for o, s in zip(IN_OFFSETS, IN_SPLITS)]
    gq, gk, gv, gr, glr, sq, sk, sv, ga, gb = seg
    glr_pad = jnp.pad(glr, ((0, 0), (0, GLR_PAD - GLA_GATE_RANK)))
    return jnp.concatenate([gq, gk, gv, gr, sq, ga, gb, sk, sv, glr_pad], axis=1).astype(BF16)


GLA_STEP = 256
GLA_HPS = 4


def _gla_kernel(q_ref, k_ref, v_ref, gr_ref, glr_ref, w2_ref, b2_ref, g_ref, cum_ref, o_ref, st_ref):
    @pl.when(pl.program_id(2) == 0)
    def _():
        st_ref[...] = jnp.zeros_like(st_ref)

    c, n = GLA_CHUNK, GLA_STEP
    row = lax.broadcasted_iota(jnp.int32, (c, c), 0)
    col = lax.broadcasted_iota(jnp.int32, (c, c), 1)
    causal = col <= row
    gain = g_ref[...]
    tril = cum_ref[...]
    glr = glr_ref[...]
    pre = []
    for hh in range(GLA_HPS):
        ks = slice(hh * GLA_DK, (hh + 1) * GLA_DK)
        q = q_ref[:, ks] * (GLA_DK ** -0.5)
        k = k_ref[:, ks]
        z = _dot(glr, w2_ref[:, ks], HIGHEST) + b2_ref[:, ks]
        log_a = jax.nn.log_sigmoid(z) / GLA_GATE_NORM
        parts = []
        for ci in range(n // c):
            bc = _dot(tril, log_a[ci * c:(ci + 1) * c], HIGHEST)
            parts.append((bc, jnp.broadcast_to(bc[c // 2:c // 2 + 1], bc.shape),
                          jnp.broadcast_to(bc[c - 1:c], bc.shape)))
        b, b_mid, b_last = (jnp.concatenate([p[i] for p in parts], axis=0) for i in range(3))
        pre.append(dict(q_in=(q * jnp.exp(b - b_mid)).astype(BF16), k_in=(k * jnp.exp(b_mid - b)).astype(BF16),
                        q_st=(q * jnp.exp(b)).astype(BF16), k_end=(k * jnp.exp(b_last - b)).astype(BF16),
                        decay=jnp.exp(b_last)))
    states = [st_ref[hh] for hh in range(GLA_HPS)]
    for ci in range(n // c):
        sl = slice(ci * c, (ci + 1) * c)
        for hh in range(GLA_HPS):
            p, st = pre[hh], states[hh]
            vs = pl.ds(hh * GLA_DV, GLA_DV)
            v = v_ref[pl.ds(ci * c, c), vs].astype(BF16)
            att = jnp.where(causal, _dot_nt(p["q_in"][sl], p["k_in"][sl]), 0.0)
            o = _dot(att.astype(BF16), v) + _dot_nt(p["q_st"][sl], st.astype(BF16))
            states[hh] = st * p["decay"][ci * c:ci * c + 1, :] + _dot_tn(v, p["k_end"][sl])
            y = o * lax.rsqrt(jnp.mean(o * o, axis=-1, keepdims=True) + NORM_EPS) * gain
            o_ref[pl.ds(ci * c, c), vs] = (y * jax.nn.silu(gr_ref[pl.ds(ci * c, c), vs])).astype(o_ref.dtype)
    for hh in range(GLA_HPS):
        st_ref[hh] = states[hh]


def _gla_cum_matrix():
    return jnp.asarray(np.tril(np.ones((GLA_CHUNK, GLA_CHUNK), np.float32)))


def gla(proj, w_gk2_pad, b_gk, gla_norm_g, bsz, seqlen):
    t = bsz * seqlen
    ns = seqlen // GLA_STEP
    rowmap = lambda b, h, s: b * ns + s
    kw, vw = GLA_HPS * GLA_DK, GLA_HPS * GLA_DV
    kq, kk = COL_GQ // kw, COL_GK // kw
    kv, kr = COL_GV // vw, COL_GR // vw
    kl = COL_GLR // LANES
    return pl.pallas_call(
        _gla_kernel,
        grid=(bsz, GLA_HEADS // GLA_HPS, ns),
        in_specs=[pl.BlockSpec((GLA_STEP, kw), lambda b, h, s: (rowmap(b, h, s), kq + h)),
                  pl.BlockSpec((GLA_STEP, kw), lambda b, h, s: (rowmap(b, h, s), kk + h)),
                  pl.BlockSpec((GLA_STEP, vw), lambda b, h, s: (rowmap(b, h, s), kv + h)),
                  pl.BlockSpec((GLA_STEP, vw), lambda b, h, s: (rowmap(b, h, s), kr + h)),
                  pl.BlockSpec((GLA_STEP, LANES), lambda b, h, s: (rowmap(b, h, s), kl)),
                  pl.BlockSpec((LANES, kw), lambda b, h, s: (0, h)),
                  pl.BlockSpec((1, kw), lambda b, h, s: (0, h)),
                  pl.BlockSpec((1, GLA_DV), lambda b, h, s: (0, 0)),
                  pl.BlockSpec((GLA_CHUNK, GLA_CHUNK), lambda b, h, s: (0, 0))],
        out_specs=pl.BlockSpec((GLA_STEP, vw), lambda b, h, s: (rowmap(b, h, s), h)),
        out_shape=jax.ShapeDtypeStruct((t, GLA_VW), BF16),
        scratch_shapes=[pltpu.VMEM((GLA_HPS, GLA_DV, GLA_DK), F32)],
        compiler_params=pltpu.CompilerParams(
            dimension_semantics=("parallel", "parallel", "arbitrary"), vmem_limit_bytes=VMEM_LIMIT),
        name="gla",
    )(proj, proj, proj, proj, proj, w_gk2_pad, b_gk.reshape(1, GLA_KW), gla_norm_g.reshape(1, GLA_DV),
      _gla_cum_matrix())


SWA_GW = SWA_GROUP * SWA_HEAD_DIM
assert SWA_KVW == SWA_GW


def _swa_kernel(sinks_ref, q_ref, k_ref, v_ref, cos_ref, sin_ref, qg_ref, kg_ref, bd_ref, rep_ref, o_ref,
                kprev_ref, vprev_ref):
    n = pl.program_id(1)

    @pl.when(n == 0)
    def _():
        kprev_ref[...] = jnp.zeros_like(kprev_ref)
        vprev_ref[...] = jnp.zeros_like(vprev_ref)

    L = SWA_BLOCK
    cos = cos_ref[...]
    sin = sin_ref[...]
    bd = bd_ref[...]
    lane = lax.broadcasted_iota(jnp.int32, (L, SWA_GW), 1)
    seg = lane // SWA_HEAD_DIM
    first_half = (lane % SWA_HEAD_DIM) < ROPE_HALF

    def norm_rope(x, gain):
        x2 = x * x
        hi = x2.astype(BF16)
        lo = (x2 - hi.astype(F32)).astype(BF16)
        ms = (_dot(hi, bd) + _dot(lo, bd)) * (1.0 / SWA_HEAD_DIM)
        y = x * lax.rsqrt(ms + NORM_EPS) * gain
        partner = jnp.where(first_half, pltpu.roll(y, SWA_GW - ROPE_HALF, 1), pltpu.roll(y, ROPE_HALF, 1))
        return y * cos + partner * sin

    rows = SWA_GROUP * L
    qi = lax.broadcasted_iota(jnp.int32, (rows, L), 0) % L
    ki = lax.broadcasted_iota(jnp.int32, (rows, L), 1)
    row_head = lax.broadcasted_iota(jnp.int32, (rows, 1), 0) // L
    mask_cur = ki <= qi
    mask_prev = ki > qi + jnp.where(n > 0, 0, L)
    neg_inf = F32(-jnp.inf)
    k_all = norm_rope(k_ref[...], kg_ref[...]).astype(BF16)
    v_all = v_ref[...].astype(BF16)
    for j in range(SWA_KV_HEADS):
        cs = pl.ds(j * SWA_GW, SWA_GW)
        q = (norm_rope(q_ref[:, cs], qg_ref[...]) * (SWA_HEAD_DIM ** -0.5)).astype(BF16)
        k_cur = _dot(k_all, rep_ref[j]).astype(BF16)
        v_cur = _dot(v_all, rep_ref[j]).astype(BF16)
        k_prev = kprev_ref[:, cs]
        v_prev = vprev_ref[:, cs]
        qs = jnp.concatenate([jnp.where(seg == g, q, jnp.zeros_like(q)) for g in range(SWA_GROUP)], axis=0)
        sink = jnp.zeros((rows, 1), F32)
        for g in range(SWA_GROUP):
            sink = jnp.where(row_head == g, sinks_ref[j * SWA_GROUP + g], sink)
        s_cur = jnp.where(mask_cur, _dot_nt(qs, k_cur), neg_inf)
        s_prev = jnp.where(mask_prev, _dot_nt(qs, k_prev), neg_inf)
        m = jnp.maximum(jnp.maximum(jnp.max(s_cur, axis=-1, keepdims=True),
                                    jnp.max(s_prev, axis=-1, keepdims=True)), sink)
        p_cur = jnp.exp(s_cur - m)
        p_prev = jnp.exp(s_prev - m)
        denom = (jnp.sum(p_cur, axis=-1, keepdims=True) + jnp.sum(p_prev, axis=-1, keepdims=True)
                 + jnp.exp(sink - m))
        og = (_dot(p_cur.astype(BF16), v_cur) + _dot(p_prev.astype(BF16), v_prev)) / denom
        acc = jnp.zeros((L, SWA_GW), F32)
        for g in range(SWA_GROUP):
            acc = acc + jnp.where(seg == g, og[g * L:(g + 1) * L], 0.0)
        o_ref[:, cs] = acc.astype(o_ref.dtype)
        kprev_ref[:, cs] = k_cur
        vprev_ref[:, cs] = v_cur


def _rope_tables(seqlen):
    pos = jnp.arange(seqlen, dtype=F32)
    inv_freq = ROPE_THETA ** (-jnp.arange(0, ROPE_DIM, 2, dtype=F32) / ROPE_DIM)
    ang = pos[:, None] * inv_freq[None, :]
    cos, sin = jnp.cos(ang), jnp.sin(ang)
    rest = SWA_HEAD_DIM - ROPE_DIM
    cos_h = jnp.concatenate([cos, cos, jnp.ones((seqlen, rest), F32)], axis=1)
    sin_h = jnp.concatenate([-sin, sin, jnp.zeros((seqlen, rest), F32)], axis=1)
    return jnp.tile(cos_h, (1, SWA_GROUP)), jnp.tile(sin_h, (1, SWA_GROUP))


def swa(proj, q_norm_g, k_norm_g, sinks, bsz, seqlen):
    t = bsz * seqlen
    nb = seqlen // SWA_BLOCK
    cos_t, sin_t = _rope_tables(seqlen)
    head = np.arange(SWA_GW) // SWA_HEAD_DIM
    blockdiag = jnp.asarray((head[:, None] == head[None, :]).astype(np.float32), dtype=BF16)
    src = np.arange(SWA_KVW)
    dst = np.arange(SWA_GW)
    rep = np.stack([(src[:, None] // SWA_HEAD_DIM == j) & (src[:, None] % SWA_HEAD_DIM == dst[None, :] % SWA_HEAD_DIM)
                    for j in range(SWA_KV_HEADS)]).astype(np.float32)
    rowmap = lambda b, n: b * nb + n
    cq, ck, cv = COL_SQ // SWA_QW, COL_SK // SWA_KVW, COL_SV // SWA_KVW
    grid_spec = pltpu.PrefetchScalarGridSpec(
        num_scalar_prefetch=1,
        grid=(bsz, nb),
        in_specs=[pl.BlockSpec((SWA_BLOCK, SWA_QW), lambda b, n, s: (rowmap(b, n), cq)),
                  pl.BlockSpec((SWA_BLOCK, SWA_KVW), lambda b, n, s: (rowmap(b, n), ck)),
                  pl.BlockSpec((SWA_BLOCK, SWA_KVW), lambda b, n, s: (rowmap(b, n), cv)),
                  pl.BlockSpec((SWA_BLOCK, SWA_GW), lambda b, n, s: (n, 0)),
                  pl.BlockSpec((SWA_BLOCK, SWA_GW), lambda b, n, s: (n, 0)),
                  pl.BlockSpec((1, SWA_GW), lambda b, n, s: (0, 0)),
                  pl.BlockSpec((1, SWA_GW), lambda b, n, s: (0, 0)),
                  pl.BlockSpec((SWA_GW, SWA_GW), lambda b, n, s: (0, 0)),
                  pl.BlockSpec((SWA_KV_HEADS, SWA_KVW, SWA_GW), lambda b, n, s: (0, 0, 0))],
        out_specs=pl.BlockSpec((SWA_BLOCK, SWA_QW), lambda b, n, s: (rowmap(b, n), 0)),
        scratch_shapes=[pltpu.VMEM((SWA_BLOCK, SWA_QW), BF16), pltpu.VMEM((SWA_BLOCK, SWA_QW), BF16)],
    )
    return pl.pallas_call(
        _swa_kernel,
        grid_spec=grid_spec,
        out_shape=jax.ShapeDtypeStruct((t, SWA_QW), BF16),
        compiler_params=pltpu.CompilerParams(
            dimension_semantics=("parallel", "arbitrary"), vmem_limit_bytes=VMEM_LIMIT),
        name="swa",
    )(sinks, proj, proj, proj, cos_t, sin_t,
      jnp.tile(q_norm_g, SWA_GROUP).reshape(1, SWA_GW), jnp.tile(k_norm_g, SWA_GROUP).reshape(1, SWA_GW),
      blockdiag, jnp.asarray(rep, dtype=BF16))


def _merge_kernel(oa_ref, ob_ref, ga_ref, gb_ref, x_ref, wa_ref, wb_ref, wo_ref, g_ref, wq_ref,
                  x1_ref, xn_ref, qp_ref):
    y_a = _dot(oa_ref[...], wa_ref[...])
    y_b = _dot(ob_ref[...], wb_ref[...])
    merged = jax.nn.sigmoid(ga_ref[...]) * y_a + jax.nn.sigmoid(gb_ref[...]) * y_b
    x1 = x_ref[...] + _dot(merged.astype(BF16), wo_ref[...])
    x1_ref[...] = x1
    xn = x1 * lax.rsqrt(jnp.mean(x1 * x1, axis=-1, keepdims=True) + NORM_EPS) * g_ref[...]
    xn_ref[...] = xn
    qp_ref[...] = _dot(xn.astype(BF16), wq_ref[...]).astype(qp_ref.dtype)


def merge(o_a, o_b, proj, x2d, w_a, w_b, w_o, norm_g, w_pq, tm=256):
    t, d = x2d.shape
    nq = w_pq.shape[1]
    ca, cb = COL_GA // D_MODEL, COL_GB // D_MODEL
    full = lambda shape: pl.BlockSpec(shape, lambda i: (0, 0))
    return pl.pallas_call(
        _merge_kernel,
        grid=(t // tm,),
        in_specs=[pl.BlockSpec((tm, GLA_VW), lambda i: (i, 0)),
                  pl.BlockSpec((tm, SWA_QW), lambda i: (i, 0)),
                  pl.BlockSpec((tm, d), lambda i: (i, ca)),
                  pl.BlockSpec((tm, d), lambda i: (i, cb)),
                  pl.BlockSpec((tm, d), lambda i: (i, 0)),
                  full((GLA_VW, d)), full((SWA_QW, d)), full((d, d)), full((1, d)), full((d, nq))],
        out_specs=[pl.BlockSpec((tm, d), lambda i: (i, 0)),
                   pl.BlockSpec((tm, d), lambda i: (i, 0)),
                   pl.BlockSpec((tm, nq), lambda i: (i, 0))],
        out_shape=[jax.ShapeDtypeStruct((t, d), F32),
                   jax.ShapeDtypeStruct((t, d), F32),
                   jax.ShapeDtypeStruct((t, nq), BF16)],
        compiler_params=pltpu.CompilerParams(
            dimension_semantics=("parallel",), vmem_limit_bytes=VMEM_LIMIT),
        name="merge",
    )(o_a, o_b, proj, proj, x2d, w_a.astype(BF16), w_b.astype(BF16), w_o.astype(BF16),
      norm_g.reshape(1, d), w_pq.astype(BF16))


PEER_TOK = LANES
PEER_HALF = PEER_QDIM // 2


def _top_rows(s, pos, payload=None):
    vals, rows = [], []
    for _ in range(PEER_TOPK):
        m = jnp.max(s, axis=0, keepdims=True)
        first = jnp.min(jnp.where(s == m, pos, F32(2 ** 24)), axis=0, keepdims=True)
        hit = pos == first
        vals.append(m)
        if payload is None:
            rows.append(first)
        else:
            rows.append(jnp.max(jnp.where(hit, payload, -1.0), axis=0, keepdims=True))
        s = jnp.where(hit, -jnp.inf, s)
    return jnp.concatenate(vals, axis=0), jnp.concatenate(rows, axis=0)


def _candidate_pieces():
    pieces = []
    single = []
    for a in range(PEER_TOPK):
        nb = PEER_TOPK // (a + 1)
        if nb >= 2:
            for b0 in range(0, nb, SUBLANES):
                pieces.append((a, 1, b0, SUBLANES))
        else:
            single.append(a)
    assert len(single) % SUBLANES == 0 and single == list(range(single[0], PEER_TOPK))
    for a0 in range(single[0], PEER_TOPK, SUBLANES):
        pieces.append((a0, SUBLANES, 0, 1))
    return pieces


def _route_subkeys(q_half, keys_half):
    key_id = lax.broadcasted_iota(jnp.int32, (PEER_NKEYS, PEER_TOK), 0).astype(F32)
    return _top_rows(_dot_nt(keys_half, q_half), key_id)


def _route_candidates(top_lo, top_hi):
    (v1, i1), (v2, i2) = top_lo, top_hi
    sub = lax.broadcasted_iota(jnp.int32, (SUBLANES, PEER_TOK), 0).astype(F32)
    pieces = _candidate_pieces()
    cpos = jnp.concatenate([(a0 + (sub if na > 1 else 0.0)) * PEER_TOPK + (b0 + (sub if nb > 1 else 0.0))
                            for a0, na, b0, nb in pieces], axis=0)
    cand = jnp.concatenate([v1[a0:a0 + na, :] + v2[b0:b0 + nb, :] for a0, na, b0, nb in pieces], axis=0)
    cidx = jnp.concatenate([i1[a0:a0 + na, :] * PEER_NKEYS + i2[b0:b0 + nb, :]
                            for a0, na, b0, nb in pieces], axis=0)
    return _top_rows(cand, cpos, cidx)


def _route_gates(best):
    e = jnp.exp(best - best[0:1, :])
    return e / jnp.sum(e, axis=0, keepdims=True)


def _route_head(q_lo, q_hi, keys_lo, keys_hi):
    best, eidx = _route_candidates(_route_subkeys(q_lo, keys_lo), _route_subkeys(q_hi, keys_hi))
    return eidx, _route_gates(best)


def _peer_topk_kernel(qp_ref, keys_ref, idx_ref, gate_ref):
    idx_rows, gate_rows = [], []
    for h in range(PEER_HEADS):
        eidx, gate_h = _route_head(qp_ref[:, pl.ds(2 * h * PEER_HALF, PEER_HALF)],
                                   qp_ref[:, pl.ds((2 * h + 1) * PEER_HALF, PEER_HALF)],
                                   keys_ref[2 * h], keys_ref[2 * h + 1])
        gate_rows.append(gate_h)
        idx_rows.append(eidx)
    gate = jnp.concatenate(gate_rows, axis=0)
    idx = jnp.concatenate(idx_rows, axis=0)
    gate_ref[...] = gate.T
    idx_ref[...] = idx.T.astype(jnp.int32)


def peer_topk(qp, sub_keys):
    t = qp.shape[0]
    keys = sub_keys.reshape(PEER_HEADS * 2, PEER_NKEYS, PEER_HALF).astype(BF16)
    return pl.pallas_call(
        _peer_topk_kernel,
        grid=(t // PEER_TOK,),
        in_specs=[pl.BlockSpec((PEER_TOK, PEER_HEADS * PEER_QDIM), lambda i: (i, 0)),
                  pl.BlockSpec((PEER_HEADS * 2, PEER_NKEYS, PEER_HALF), lambda i: (0, 0, 0))],
        out_specs=[pl.BlockSpec((PEER_TOK, PEER_HK), lambda i: (i, 0)),
                   pl.BlockSpec((PEER_TOK, PEER_HK), lambda i: (i, 0))],
        out_shape=[jax.ShapeDtypeStruct((t, PEER_HK), jnp.int32),
                   jax.ShapeDtypeStruct((t, PEER_HK), F32)],
        compiler_params=pltpu.CompilerParams(
            dimension_semantics=("arbitrary",), vmem_limit_bytes=VMEM_LIMIT),
        name="peer_topk",
    )(qp, keys)


PEXP_NBUF = 16
PEXP_SET = 4
PEXP_DMA_QUEUES = 2
PEXP_DT = D_MODEL // LANES
PEXP_ROW = 2 * PEXP_DT
PEXP_PITCH = PEXP_ROW + 1


def _peer_exp_kernel(idx_ref, nidx_ref, xn_ref, gate_ref, x1_ref, uv_ref, o_ref, *scratch):
    bufs, sem_ref = scratch[:PEXP_NBUF], scratch[PEXP_NBUF]
    sets = PEXP_NBUF // PEXP_SET
    step = pl.program_id(0)
    last_step = pl.num_programs(0) - 1

    def row_copy(src_row, slot, k):
        return pltpu.make_async_copy(uv_ref.at[src_row], bufs[slot].at[pl.ds(k * PEXP_PITCH, PEXP_ROW)],
                                     sem_ref.at[slot])

    def start(ids_ref, slot):
        for k in range(PEER_HK):
            row_copy(ids_ref[slot, k], slot, k).start(priority=k % PEXP_DMA_QUEUES)

    def wait(slot):
        for k in range(PEER_HK):
            row_copy(0, slot, k).wait()

    def hidden(slot):
        x = xn_ref[pl.ds(slot, 1), :].astype(BF16)
        h = jnp.zeros((1, PEER_HK), F32)
        for c in range(PEXP_DT):
            u_c = bufs[slot][pl.ds(c, PEER_HK, stride=PEXP_PITCH), :].astype(BF16)
            h = h + _dot_nt(x[:, c * LANES:(c + 1) * LANES], u_c)
        return (gate_ref[pl.ds(slot, 1), :] * _gelu_exact(h)).astype(BF16)

    def project(slot, w):
        out = [_dot(w, bufs[slot][pl.ds(PEXP_DT + c, PEER_HK, stride=PEXP_PITCH), :].astype(BF16))
               for c in range(PEXP_DT)]
        o_ref[pl.ds(slot, 1), :] = x1_ref[pl.ds(slot, 1), :] + jnp.concatenate(out, axis=1)

    @pl.when(step == 0)
    def _():
        for slot in range(PEXP_NBUF - PEXP_SET):
            start(idx_ref, slot)

    for s in range(sets):
        slots = range(s * PEXP_SET, (s + 1) * PEXP_SET)
        for slot in slots:
            wait(slot)
        if s == 0:
            for slot in range((sets - 1) * PEXP_SET, PEXP_NBUF):
                start(idx_ref, slot)
        else:
            for slot in range((s - 1) * PEXP_SET, s * PEXP_SET):
                start(nidx_ref, slot)
        ws = [hidden(slot) for slot in slots]
        for slot, w in zip(slots, ws):
            project(slot, w)

    @pl.when(step == last_step)
    def _():
        for slot in range(PEXP_NBUF - PEXP_SET):
            wait(slot)


def peer_exp(idx, xn, gate, x1, uv):
    t, d = xn.shape
    steps = t // PEXP_NBUF
    tok_block = lambda width: pl.BlockSpec((PEXP_NBUF, width), lambda i: (i, 0))
    return pl.pallas_call(
        _peer_exp_kernel,
        grid=(steps,),
        in_specs=[pl.BlockSpec((PEXP_NBUF, PEER_HK), lambda i: (i, 0), memory_space=pltpu.SMEM),
                  pl.BlockSpec((PEXP_NBUF, PEER_HK), lambda i: (jnp.minimum(i + 1, steps - 1), 0),
                               memory_space=pltpu.SMEM),
                  tok_block(d), tok_block(PEER_HK), tok_block(d),
                  pl.BlockSpec(memory_space=pl.ANY)],
        out_specs=tok_block(d),
        out_shape=jax.ShapeDtypeStruct((t, d), F32),
        scratch_shapes=[pltpu.VMEM((PEER_HK * PEXP_PITCH, LANES), F32)] * PEXP_NBUF
                       + [pltpu.SemaphoreType.DMA((PEXP_NBUF,))],
        compiler_params=pltpu.CompilerParams(
            dimension_semantics=("arbitrary",), vmem_limit_bytes=VMEM_LIMIT),
        name="peer_exp",
    )(idx, idx, xn, gate, x1, uv)


def kernel(x, norm_mix_g, w_in, w_gk2, b_gk, gla_norm_g, q_norm_g, k_norm_g, attn_sinks, w_branch_a, w_branch_b, w_out, norm_ffn_g, w_peer_q, peer_sub_keys, peer_u, peer_v):
    bsz, seqlen, d = x.shape
    t = bsz * seqlen
    depth = w_in.shape[0]
    x2d = x.reshape(t, d)
    for l in range(depth):
        proj = in_proj(x2d, norm_mix_g[l], _relayout_w_in(w_in[l]))
        w_gk2_pad = jnp.pad(w_gk2[l], ((0, LANES - GLA_GATE_RANK), (0, 0)))
        o_a = gla(proj, w_gk2_pad, b_gk[l], gla_norm_g[l], bsz, seqlen)
        o_b = swa(proj, q_norm_g[l], k_norm_g[l], attn_sinks[l], bsz, seqlen)
        x1, xn, qp = merge(o_a, o_b, proj, x2d, w_branch_a[l], w_branch_b[l], w_out[l],
                           norm_ffn_g[l], w_peer_q[l])
        idx, gate = peer_topk(qp, peer_sub_keys[l])
        uv = jnp.concatenate([peer_u[l], peer_v[l]], axis=1).reshape(-1, PEXP_ROW, LANES)
        x2d = peer_exp(idx, xn, gate, x1, uv)
    return x2d.reshape(bsz, seqlen, d)
```

```python
import jax
import jax.numpy as jnp
import numpy as np
from jax import lax
from jax.experimental import pallas as pl
from jax.experimental.pallas import tpu as pltpu

D_MODEL = 1024
GLA_HEADS = 4
GLA_DK = 128
GLA_DV = 256
GLA_GATE_RANK = 16
GLA_GATE_NORM = 16.0
GLA_CHUNK = 64
SWA_HEADS = 16
SWA_KV_HEADS = 4
SWA_GROUP = SWA_HEADS // SWA_KV_HEADS
SWA_HEAD_DIM = 64
SWA_WINDOW = 128
SWA_BLOCK = 128
assert SWA_WINDOW == SWA_BLOCK
ROPE_THETA = 500000.0
ROPE_DIM = SWA_HEAD_DIM // 4
ROPE_HALF = ROPE_DIM // 2
PEER_HEADS = 8
PEER_NKEYS = 128
PEER_QDIM = 256
PEER_TOPK = 16
PEER_HK = PEER_HEADS * PEER_TOPK
NORM_EPS = 1e-6

GLA_KW = GLA_HEADS * GLA_DK
GLA_VW = GLA_HEADS * GLA_DV
SWA_QW = SWA_HEADS * SWA_HEAD_DIM
SWA_KVW = SWA_KV_HEADS * SWA_HEAD_DIM
IN_SPLITS = (GLA_KW, GLA_KW, GLA_VW, GLA_VW, GLA_GATE_RANK, SWA_QW, SWA_KVW, SWA_KVW, D_MODEL, D_MODEL)
IN_OFFSETS = tuple(int(v) for v in np.cumsum((0,) + IN_SPLITS)[:-1])

LANES = 128
SUBLANES = 8
COL_GQ = 0
COL_GK = COL_GQ + GLA_KW
COL_GV = COL_GK + GLA_KW
COL_GR = COL_GV + GLA_VW
COL_SQ = COL_GR + GLA_VW
COL_GA = COL_SQ + SWA_QW
COL_GB = COL_GA + D_MODEL
COL_SK = COL_GB + D_MODEL
COL_SV = COL_SK + SWA_KVW
COL_GLR = COL_SV + SWA_KVW
GLR_PAD = 2 * LANES
PROJ_W = COL_GLR + GLR_PAD

VMEM_LIMIT = 48 * 1024 * 1024

F32 = jnp.float32
BF16 = jnp.bfloat16
HIGHEST = lax.Precision.HIGHEST


def _dot(a, b, precision=None):
    return jnp.dot(a, b, preferred_element_type=F32, precision=precision)


def _dot_nt(a, b):
    return lax.dot_general(a, b, (((1,), (1,)), ((), ())), preferred_element_type=F32)


def _gelu_exact(x):
    return 0.5 * x * (1.0 + lax.erf(x * (2.0 ** -0.5)))


def _dot_tn(a, b):
    return lax.dot_general(a, b, (((0,), (0,)), ((), ())), preferred_element_type=F32)


def _in_proj_kernel(x_ref, g_ref, w_ref, o_ref, xn_ref):
    @pl.when(pl.program_id(1) == 0)
    def _():
        x = x_ref[...]
        y = x * lax.rsqrt(jnp.mean(x * x, axis=-1, keepdims=True) + NORM_EPS) * g_ref[...]
        xn_ref[...] = y.astype(BF16)

    o_ref[...] = _dot(xn_ref[...], w_ref[...])


def in_proj(x2d, g, w_bf16, tm=1024, tn=2304):
    t, d = x2d.shape
    n = w_bf16.shape[1]
    return pl.pallas_call(
        _in_proj_kernel,
        grid=(t // tm, n // tn),
        in_specs=[pl.BlockSpec((tm, d), lambda i, j: (i, 0)),
                  pl.BlockSpec((1, d), lambda i, j: (0, 0)),
                  pl.BlockSpec((d, tn), lambda i, j: (0, j))],
        out_specs=pl.BlockSpec((tm, tn), lambda i, j: (i, j)),
        out_shape=jax.ShapeDtypeStruct((t, n), F32),
        scratch_shapes=[pltpu.VMEM((tm, d), BF16)],
        compiler_params=pltpu.CompilerParams(
            dimension_semantics=("parallel", "arbitrary"), vmem_limit_bytes=VMEM_LIMIT),
        name="in_proj",
    )(x2d, g.reshape(1, d), w_bf16)


def _relayout_w_in(w_in):
    seg = [w_in[:, o:o + s] for o, s in zip(IN_OFFSETS, IN_SPLITS)]
    gq, gk, gv, gr, glr, sq, sk, sv, ga, gb = seg
    glr_pad = jnp.pad(glr, ((0, 0), (0, GLR_PAD - GLA_GATE_RANK)))
    return jnp.concatenate([gq, gk, gv, gr, sq, ga, gb, sk, sv, glr_pad], axis=1).astype(BF16)


GLA_STEP = 512
GLA_HPS = 4


def _gla_kernel(q_ref, k_ref, v_ref, gr_ref, glr_ref, w2_ref, b2_ref, g_ref, cum_ref, o_ref, st_ref):
    @pl.when(pl.program_id(2) == 0)
    def _():
        st_ref[...] = jnp.zeros_like(st_ref)

    c, n = GLA_CHUNK, GLA_STEP
    row = lax.broadcasted_iota(jnp.int32, (c, c), 0)
    col = lax.broadcasted_iota(jnp.int32, (c, c), 1)
    causal = col <= row
    gain = g_ref[...]
    tril = cum_ref[...]
    glr = glr_ref[...]
    pre = []
    for hh in range(GLA_HPS):
        ks = slice(hh * GLA_DK, (hh + 1) * GLA_DK)
        q = q_ref[:, ks] * (GLA_DK ** -0.5)
        k = k_ref[:, ks]
        z = _dot(glr, w2_ref[:, ks], HIGHEST) + b2_ref[:, ks]
        log_a = jax.nn.log_sigmoid(z) / GLA_GATE_NORM
        parts = []
        for ci in range(n // c):
            bc = _dot(tril, log_a[ci * c:(ci + 1) * c], HIGHEST)
            parts.append((bc, jnp.broadcast_to(bc[c // 2:c // 2 + 1], bc.shape),
                          jnp.broadcast_to(bc[c - 1:c], bc.shape)))
        b, b_mid, b_last = (jnp.concatenate([p[i] for p in parts], axis=0) for i in range(3))
        pre.append(dict(q_in=(q * jnp.exp(b - b_mid)).astype(BF16), k_in=(k * jnp.exp(b_mid - b)).astype(BF16),
                        q_st=(q * jnp.exp(b)).astype(BF16), k_end=(k * jnp.exp(b_last - b)).astype(BF16),
                        decay=jnp.exp(b_last)))
    states = [st_ref[hh] for hh in range(GLA_HPS)]
    for ci in range(n // c):
        sl = slice(ci * c, (ci + 1) * c)
        for hh in range(GLA_HPS):
            p, st = pre[hh], states[hh]
            vs = pl.ds(hh * GLA_DV, GLA_DV)
            v = v_ref[pl.ds(ci * c, c), vs].astype(BF16)
            att = jnp.where(causal, _dot_nt(p["q_in"][sl], p["k_in"][sl]), 0.0)
            o = _dot(att.astype(BF16), v) + _dot_nt(p["q_st"][sl], st.astype(BF16))
            states[hh] = st * p["decay"][ci * c:ci * c + 1, :] + _dot_tn(v, p["k_end"][sl])
            y = o * lax.rsqrt(jnp.mean(o * o, axis=-1, keepdims=True) + NORM_EPS) * gain
            o_ref[pl.ds(ci * c, c), vs] = (y * jax.nn.silu(gr_ref[pl.ds(ci * c, c), vs])).astype(o_ref.dtype)
    for hh in range(GLA_HPS):
        st_ref[hh] = states[hh]


def _gla_cum_matrix():
    return jnp.asarray(np.tril(np.ones((GLA_CHUNK, GLA_CHUNK), np.float32)))


def gla(proj, w_gk2_pad, b_gk, gla_norm_g, bsz, seqlen):
    t = bsz * seqlen
    ns = seqlen // GLA_STEP
    rowmap = lambda b, h, s: b * ns + s
    kw, vw = GLA_HPS * GLA_DK, GLA_HPS * GLA_DV
    kq, kk = COL_GQ // kw, COL_GK // kw
    kv, kr = COL_GV // vw, COL_GR // vw
    kl = COL_GLR // LANES
    return pl.pallas_call(
        _gla_kernel,
        grid=(bsz, GLA_HEADS // GLA_HPS, ns),
        in_specs=[pl.BlockSpec((GLA_STEP, kw), lambda b, h, s: (rowmap(b, h, s), kq + h)),
                  pl.BlockSpec((GLA_STEP, kw), lambda b, h, s: (rowmap(b, h, s), kk + h)),
                  pl.BlockSpec((GLA_STEP, vw), lambda b, h, s: (rowmap(b, h, s), kv + h)),
                  pl.BlockSpec((GLA_STEP, vw), lambda b, h, s: (rowmap(b, h, s), kr + h)),
                  pl.BlockSpec((GLA_STEP, LANES), lambda b, h, s: (rowmap(b, h, s), kl)),
                  pl.BlockSpec((LANES, kw), lambda b, h, s: (0, h)),
                  pl.BlockSpec((1, kw), lambda b, h, s: (0, h)),
                  pl.BlockSpec((1, GLA_DV), lambda b, h, s: (0, 0)),
                  pl.BlockSpec((GLA_CHUNK, GLA_CHUNK), lambda b, h, s: (0, 0))],
        out_specs=pl.BlockSpec((GLA_STEP, vw), lambda b, h, s: (rowmap(b, h, s), h)),
        out_shape=jax.ShapeDtypeStruct((t, GLA_VW), BF16),
        scratch_shapes=[pltpu.VMEM((GLA_HPS, GLA_DV, GLA_DK), F32)],
        compiler_params=pltpu.CompilerParams(
            dimension_semantics=("parallel", "parallel", "arbitrary"), vmem_limit_bytes=VMEM_LIMIT),
        name="gla",
    )(proj, proj, proj, proj, proj, w_gk2_pad, b_gk.reshape(1, GLA_KW), gla_norm_g.reshape(1, GLA_DV),
      _gla_cum_matrix())


SWA_GW = SWA_GROUP * SWA_HEAD_DIM
assert SWA_KVW == SWA_GW


def _swa_kernel(sinks_ref, q_ref, k_ref, v_ref, cos_ref, sin_ref, qg_ref, kg_ref, bd_ref, rep_ref, o_ref,
                kprev_ref, vprev_ref):
    n = pl.program_id(1)

    @pl.when(n == 0)
    def _():
        kprev_ref[...] = jnp.zeros_like(kprev_ref)
        vprev_ref[...] = jnp.zeros_like(vprev_ref)

    L = SWA_BLOCK
    cos = cos_ref[...]
    sin = sin_ref[...]
    bd = bd_ref[...]
    lane = lax.broadcasted_iota(jnp.int32, (L, SWA_GW), 1)
    seg = lane // SWA_HEAD_DIM
    first_half = (lane % SWA_HEAD_DIM) < ROPE_HALF

    def norm_rope(x, gain):
        x2 = x * x
        hi = x2.astype(BF16)
        lo = (x2 - hi.astype(F32)).astype(BF16)
        ms = (_dot(hi, bd) + _dot(lo, bd)) * (1.0 / SWA_HEAD_DIM)
        y = x * lax.rsqrt(ms + NORM_EPS) * gain
        partner = jnp.where(first_half, pltpu.roll(y, SWA_GW - ROPE_HALF, 1), pltpu.roll(y, ROPE_HALF, 1))
        return y * cos + partner * sin

    rows = SWA_GROUP * L
    qi = lax.broadcasted_iota(jnp.int32, (rows, L), 0) % L
    ki = lax.broadcasted_iota(jnp.int32, (rows, L), 1)
    row_head = lax.broadcasted_iota(jnp.int32, (rows, 1), 0) // L
    mask_cur = ki <= qi
    mask_prev = ki > qi + jnp.where(n > 0, 0, L)
    neg_inf = F32(-jnp.inf)
    k_all = norm_rope(k_ref[...], kg_ref[...]).astype(BF16)
    v_all = v_ref[...].astype(BF16)
    for j in range(SWA_KV_HEADS):
        cs = pl.ds(j * SWA_GW, SWA_GW)
        q = (norm_rope(q_ref[:, cs], qg_ref[...]) * (SWA_HEAD_DIM ** -0.5)).astype(BF16)
        k_cur = _dot(k_all, rep_ref[j]).astype(BF16)
        v_cur = _dot(v_all, rep_ref[j]).astype(BF16)
        k_prev = kprev_ref[:, cs]
        v_prev = vprev_ref[:, cs]
        qs = jnp.concatenate([jnp.where(seg == g, q, jnp.zeros_like(q)) for g in range(SWA_GROUP)], axis=0)
        sink = jnp.zeros((rows, 1), F32)
        for g in range(SWA_GROUP):
            sink = jnp.where(row_head == g, sinks_ref[j * SWA_GROUP + g], sink)
        s_cur = jnp.where(mask_cur, _dot_nt(qs, k_cur), neg_inf)
        s_prev = jnp.where(mask_prev, _dot_nt(qs, k_prev), neg_inf)
        m = jnp.maximum(jnp.maximum(jnp.max(s_cur, axis=-1, keepdims=True),
                                    jnp.max(s_prev, axis=-1, keepdims=True)), sink)
        p_cur = jnp.exp(s_cur - m)
        p_prev = jnp.exp(s_prev - m)
        denom = (jnp.sum(p_cur, axis=-1, keepdims=True) + jnp.sum(p_prev, axis=-1, keepdims=True)
                 + jnp.exp(sink - m))
        og = (_dot(p_cur.astype(BF16), v_cur) + _dot(p_prev.astype(BF16), v_prev)) / denom
        acc = jnp.zeros((L, SWA_GW), F32)
        for g in range(SWA_GROUP):
            acc = acc + jnp.where(seg == g, og[g * L:(g + 1) * L], 0.0)
        o_ref[:, cs] = acc.astype(o_ref.dtype)
        kprev_ref[:, cs] = k_cur
        vprev_ref[:, cs] = v_cur


def _rope_tables(seqlen):
    pos = jnp.arange(seqlen, dtype=F32)
    inv_freq = ROPE_THETA ** (-jnp.arange(0, ROPE_DIM, 2, dtype=F32) / ROPE_DIM)
    ang = pos[:, None] * inv_freq[None, :]
    cos, sin = jnp.cos(ang), jnp.sin(ang)
    rest = SWA_HEAD_DIM - ROPE_DIM
    cos_h = jnp.concatenate([cos, cos, jnp.ones((seqlen, rest), F32)], axis=1)
    sin_h = jnp.concatenate([-sin, sin, jnp.zeros((seqlen, rest), F32)], axis=1)
    return jnp.tile(cos_h, (1, SWA_GROUP)), jnp.tile(sin_h, (1, SWA_GROUP))


def swa(proj, q_norm_g, k_norm_g, sinks, bsz, seqlen):
    t = bsz * seqlen
    nb = seqlen // SWA_BLOCK
    cos_t, sin_t = _rope_tables(seqlen)
    head = np.arange(SWA_GW) // SWA_HEAD_DIM
    blockdiag = jnp.asarray((head[:, None] == head[None, :]).astype(np.float32), dtype=BF16)
    src = np.arange(SWA_KVW)
    dst = np.arange(SWA_GW)
    rep = np.stack([(src[:, None] // SWA_HEAD_DIM == j) & (src[:, None] % SWA_HEAD_DIM == dst[None, :] % SWA_HEAD_DIM)
                    for j in range(SWA_KV_HEADS)]).astype(np.float32)
    rowmap = lambda b, n: b * nb + n
    cq, ck, cv = COL_SQ // SWA_QW, COL_SK // SWA_KVW, COL_SV // SWA_KVW
    grid_spec = pltpu.PrefetchScalarGridSpec(
        num_scalar_prefetch=1,
        grid=(bsz, nb),
        in_specs=[pl.BlockSpec((SWA_BLOCK, SWA_QW), lambda b, n, s: (rowmap(b, n), cq)),
                  pl.BlockSpec((SWA_BLOCK, SWA_KVW), lambda b, n, s: (rowmap(b, n), ck)),
                  pl.BlockSpec((SWA_BLOCK, SWA_KVW), lambda b, n, s: (rowmap(b, n), cv)),
                  pl.BlockSpec((SWA_BLOCK, SWA_GW), lambda b, n, s: (n, 0)),
                  pl.BlockSpec((SWA_BLOCK, SWA_GW), lambda b, n, s: (n, 0)),
                  pl.BlockSpec((1, SWA_GW), lambda b, n, s: (0, 0)),
                  pl.BlockSpec((1, SWA_GW), lambda b, n, s: (0, 0)),
                  pl.BlockSpec((SWA_GW, SWA_GW), lambda b, n, s: (0, 0)),
                  pl.BlockSpec((SWA_KV_HEADS, SWA_KVW, SWA_GW), lambda b, n, s: (0, 0, 0))],
        out_specs=pl.BlockSpec((SWA_BLOCK, SWA_QW), lambda b, n, s: (rowmap(b, n), 0)),
        scratch_shapes=[pltpu.VMEM((SWA_BLOCK, SWA_QW), BF16), pltpu.VMEM((SWA_BLOCK, SWA_QW), BF16)],
    )
    return pl.pallas_call(
        _swa_kernel,
        grid_spec=grid_spec,
        out_shape=jax.ShapeDtypeStruct((t, SWA_QW), BF16),
        compiler_params=pltpu.CompilerParams(
            dimension_semantics=("parallel", "arbitrary"), vmem_limit_bytes=VMEM_LIMIT),
        name="swa",
    )(sinks, proj, proj, proj, cos_t, sin_t,
      jnp.tile(q_norm_g, SWA_GROUP).reshape(1, SWA_GW), jnp.tile(k_norm_g, SWA_GROUP).reshape(1, SWA_GW),
      blockdiag, jnp.asarray(rep, dtype=BF16))


def _merge_kernel(oa_ref, ob_ref, ga_ref, gb_ref, x_ref, wa_ref, wb_ref, wo_ref, g_ref, wq_ref,
                  x1_ref, xn_ref, qp_ref):
    y_a = _dot(oa_ref[...], wa_ref[...])
    y_b = _dot(ob_ref[...], wb_ref[...])
    merged = jax.nn.sigmoid(ga_ref[...]) * y_a + jax.nn.sigmoid(gb_ref[...]) * y_b
    x1 = x_ref[...] + _dot(merged.astype(BF16), wo_ref[...])
    x1_ref[...] = x1
    xn = x1 * lax.rsqrt(jnp.mean(x1 * x1, axis=-1, keepdims=True) + NORM_EPS) * g_ref[...]
    xn_ref[...] = xn
    qp_ref[...] = _dot(xn.astype(BF16), wq_ref[...]).astype(qp_ref.dtype)


def merge(o_a, o_b, proj, x2d, w_a, w_b, w_o, norm_g, w_pq, tm=256):
    t, d = x2d.shape
    nq = w_pq.shape[1]
    ca, cb = COL_GA // D_MODEL, COL_GB // D_MODEL
    full = lambda shape: pl.BlockSpec(shape, lambda i: (0, 0))
    return pl.pallas_call(
        _merge_kernel,
        grid=(t // tm,),
        in_specs=[pl.BlockSpec((tm, GLA_VW), lambda i: (i, 0)),
                  pl.BlockSpec((tm, SWA_QW), lambda i: (i, 0)),
                  pl.BlockSpec((tm, d), lambda i: (i, ca)),
                  pl.BlockSpec((tm, d), lambda i: (i, cb)),
                  pl.BlockSpec((tm, d), lambda i: (i, 0)),
                  full((GLA_VW, d)), full((SWA_QW, d)), full((d, d)), full((1, d)), full((d, nq))],
        out_specs=[pl.BlockSpec((tm, d), lambda i: (i, 0)),
                   pl.BlockSpec((tm, d), lambda i: (i, 0)),
                   pl.BlockSpec((tm, nq), lambda i: (i, 0))],
        out_shape=[jax.ShapeDtypeStruct((t, d), F32),
                   jax.ShapeDtypeStruct((t, d), F32),
                   jax.ShapeDtypeStruct((t, nq), BF16)],
        compiler_params=pltpu.CompilerParams(
            dimension_semantics=("parallel",), vmem_limit_bytes=VMEM_LIMIT),
        name="merge",
    )(o_a, o_b, proj, proj, x2d, w_a.astype(BF16), w_b.astype(BF16), w_o.astype(BF16),
      norm_g.reshape(1, d), w_pq.astype(BF16))


PEER_TOK = 2 * LANES
PEER_HALF = PEER_QDIM // 2


def _top_rows(s, pos, payload=None):
    vals, rows = [], []
    for _ in range(PEER_TOPK):
        m = jnp.max(s, axis=0, keepdims=True)
        first = jnp.min(jnp.where(s == m, pos, F32(2 ** 24)), axis=0, keepdims=True)
        hit = pos == first
        vals.append(m)
        if payload is None:
            rows.append(first)
        else:
            rows.append(jnp.max(jnp.where(hit, payload, -1.0), axis=0, keepdims=True))
        s = jnp.where(hit, -jnp.inf, s)
    return jnp.concatenate(vals, axis=0), jnp.concatenate(rows, axis=0)


def _candidate_pieces():
    pieces = []
    single = []
    for a in range(PEER_TOPK):
        nb = PEER_TOPK // (a + 1)
        if nb >= 2:
            for b0 in range(0, nb, SUBLANES):
                pieces.append((a, 1, b0, SUBLANES))
        else:
            single.append(a)
    assert len(single) % SUBLANES == 0 and single == list(range(single[0], PEER_TOPK))
    for a0 in range(single[0], PEER_TOPK, SUBLANES):
        pieces.append((a0, SUBLANES, 0, 1))
    return pieces


def _route_subkeys(q_half, keys_half):
    key_id = lax.broadcasted_iota(jnp.int32, (PEER_NKEYS, PEER_TOK), 0).astype(F32)
    return _top_rows(_dot_nt(keys_half, q_half), key_id)


def _route_candidates(top_lo, top_hi):
    (v1, i1), (v2, i2) = top_lo, top_hi
    sub = lax.broadcasted_iota(jnp.int32, (SUBLANES, PEER_TOK), 0).astype(F32)
    pieces = _candidate_pieces()
    cpos = jnp.concatenate([(a0 + (sub if na > 1 else 0.0)) * PEER_TOPK + (b0 + (sub if nb > 1 else 0.0))
                            for a0, na, b0, nb in pieces], axis=0)
    cand = jnp.concatenate([v1[a0:a0 + na, :] + v2[b0:b0 + nb, :] for a0, na, b0, nb in pieces], axis=0)
    cidx = jnp.concatenate([i1[a0:a0 + na, :] * PEER_NKEYS + i2[b0:b0 + nb, :]
                            for a0, na, b0, nb in pieces], axis=0)
    return _top_rows(cand, cpos, cidx)


def _route_gates(best):
    e = jnp.exp(best - best[0:1, :])
    return e / jnp.sum(e, axis=0, keepdims=True)


def _route_head(q_lo, q_hi, keys_lo, keys_hi):
    best, eidx = _route_candidates(_route_subkeys(q_lo, keys_lo), _route_subkeys(q_hi, keys_hi))
    return eidx, _route_gates(best)


def _peer_topk_kernel(qp_ref, keys_ref, idx_ref, gate_ref):
    idx_rows, gate_rows = [], []
    for h in range(PEER_HEADS):
        eidx, gate_h = _route_head(qp_ref[:, pl.ds(2 * h * PEER_HALF, PEER_HALF)],
                                   qp_ref[:, pl.ds((2 * h + 1) * PEER_HALF, PEER_HALF)],
                                   keys_ref[2 * h], keys_ref[2 * h + 1])
        gate_rows.append(gate_h)
        idx_rows.append(eidx)
    gate = jnp.concatenate(gate_rows, axis=0)
    idx = jnp.concatenate(idx_rows, axis=0)
    gate_ref[...] = gate.T
    idx_ref[...] = idx.T.astype(jnp.int32)


def peer_topk(qp, sub_keys):
    t = qp.shape[0]
    keys = sub_keys.reshape(PEER_HEADS * 2, PEER_NKEYS, PEER_HALF).astype(BF16)
    return pl.pallas_call(
        _peer_topk_kernel,
        grid=(t // PEER_TOK,),
        in_specs=[pl.BlockSpec((PEER_TOK, PEER_HEADS * PEER_QDIM), lambda i: (i, 0)),
                  pl.BlockSpec((PEER_HEADS * 2, PEER_NKEYS, PEER_HALF), lambda i: (0, 0, 0))],
        out_specs=[pl.BlockSpec((PEER_TOK, PEER_HK), lambda i: (i, 0)),
                   pl.BlockSpec((PEER_TOK, PEER_HK), lambda i: (i, 0))],
        out_shape=[jax.ShapeDtypeStruct((t, PEER_HK), jnp.int32),
                   jax.ShapeDtypeStruct((t, PEER_HK), F32)],
        compiler_params=pltpu.CompilerParams(
            dimension_semantics=("arbitrary",), vmem_limit_bytes=VMEM_LIMIT),
        name="peer_topk",
    )(qp, keys)


PEXP_NBUF = 16
PEXP_SET = 4
PEXP_DMA_QUEUES = 2
PEXP_DT = D_MODEL // LANES
PEXP_ROW = 2 * PEXP_DT
PEXP_PITCH = PEXP_ROW + 1


def _peer_exp_kernel(idx_ref, nidx_ref, xn_ref, gate_ref, x1_ref, uv_ref, o_ref, *scratch):
    bufs, sem_ref = scratch[:PEXP_NBUF], scratch[PEXP_NBUF]
    sets = PEXP_NBUF // PEXP_SET
    step = pl.program_id(0)
    last_step = pl.num_programs(0) - 1

    def row_copy(src_row, slot, k):
        return pltpu.make_async_copy(uv_ref.at[src_row], bufs[slot].at[pl.ds(k * PEXP_PITCH, PEXP_ROW)],
                                     sem_ref.at[slot])

    def start(ids_ref, slot):
        for k in range(PEER_HK):
            row_copy(ids_ref[slot, k], slot, k).start(priority=k % PEXP_DMA_QUEUES)

    def wait(slot):
        for k in range(PEER_HK):
            row_copy(0, slot, k).wait()

    def hidden(slot):
        x = xn_ref[pl.ds(slot, 1), :].astype(BF16)
        h = jnp.zeros((1, PEER_HK), F32)
        for c in range(PEXP_DT):
            u_c = bufs[slot][pl.ds(c, PEER_HK, stride=PEXP_PITCH), :].astype(BF16)
            h = h + _dot_nt(x[:, c * LANES:(c + 1) * LANES], u_c)
        return (gate_ref[pl.ds(slot, 1), :] * _gelu_exact(h)).astype(BF16)

    def project(slot, w):
        out = [_dot(w, bufs[slot][pl.ds(PEXP_DT + c, PEER_HK, stride=PEXP_PITCH), :].astype(BF16))
               for c in range(PEXP_DT)]
        o_ref[pl.ds(slot, 1), :] = x1_ref[pl.ds(slot, 1), :] + jnp.concatenate(out, axis=1)

    @pl.when(step == 0)
    def _():
        for slot in range(PEXP_NBUF - PEXP_SET):
            start(idx_ref, slot)

    for s in range(sets):
        slots = range(s * PEXP_SET, (s + 1) * PEXP_SET)
        for slot in slots:
            wait(slot)
        if s == 0:
            for slot in range((sets - 1) * PEXP_SET, PEXP_NBUF):
                start(idx_ref, slot)
        else:
            for slot in range((s - 1) * PEXP_SET, s * PEXP_SET):
                start(nidx_ref, slot)
        ws = [hidden(slot) for slot in slots]
        for slot, w in zip(slots, ws):
            project(slot, w)

    @pl.when(step == last_step)
    def _():
        for slot in range(PEXP_NBUF - PEXP_SET):
            wait(slot)


def peer_exp(idx, xn, gate, x1, uv):
    t, d = xn.shape
    steps = t // PEXP_NBUF
    tok_block = lambda width: pl.BlockSpec((PEXP_NBUF, width), lambda i: (i, 0))
    return pl.pallas_call(
        _peer_exp_kernel,
        grid=(steps,),
        in_specs=[pl.BlockSpec((PEXP_NBUF, PEER_HK), lambda i: (i, 0), memory_space=pltpu.SMEM),
                  pl.BlockSpec((PEXP_NBUF, PEER_HK), lambda i: (jnp.minimum(i + 1, steps - 1), 0),
                               memory_space=pltpu.SMEM),
                  tok_block(d), tok_block(PEER_HK), tok_block(d),
                  pl.BlockSpec(memory_space=pl.ANY)],
        out_specs=tok_block(d),
        out_shape=jax.ShapeDtypeStruct((t, d), F32),
        scratch_shapes=[pltpu.VMEM((PEER_HK * PEXP_PITCH, LANES), F32)] * PEXP_NBUF
                       + [pltpu.SemaphoreType.DMA((PEXP_NBUF,))],
        compiler_params=pltpu.CompilerParams(
            dimension_semantics=("arbitrary",), vmem_limit_bytes=VMEM_LIMIT),
        name="peer_exp",
    )(idx, idx, xn, gate, x1, uv)


def kernel(x, norm_mix_g, w_in, w_gk2, b_gk, gla_norm_g, q_norm_g, k_norm_g, attn_sinks, w_branch_a, w_branch_b, w_out, norm_ffn_g, w_peer_q, peer_sub_keys, peer_u, peer_v):
    bsz, seqlen, d = x.shape
    t = bsz * seqlen
    depth = w_in.shape[0]
    x2d = x.reshape(t, d)
    for l in range(depth):
        proj = in_proj(x2d, norm_mix_g[l], _relayout_w_in(w_in[l]))
        w_gk2_pad = jnp.pad(w_gk2[l], ((0, LANES - GLA_GATE_RANK), (0, 0)))
        o_a = gla(proj, w_gk2_pad, b_gk[l], gla_norm_g[l], bsz, seqlen)
        o_b = swa(proj, q_norm_g[l], k_norm_g[l], attn_sinks[l], bsz, seqlen)
        x1, xn, qp = merge(o_a, o_b, proj, x2d, w_branch_a[l], w_branch_b[l], w_out[l],
                           norm_ffn_g[l], w_peer_q[l])
        idx, gate = peer_topk(qp, peer_sub_keys[l])
        uv = jnp.concatenate([peer_u[l], peer_v[l]], axis=1).reshape(-1, PEXP_ROW, LANES)
        x2d = peer_exp(idx, xn, gate, x1, uv)
    return x2d.reshape(bsz, seqlen, d)
```

```python
import jax
import jax.numpy as jnp
import numpy as np
from jax import lax
from jax.experimental import pallas as pl
from jax.experimental.pallas import tpu as pltpu

D_MODEL = 1024
GLA_HEADS = 4
GLA_DK = 128
GLA_DV = 256
GLA_GATE_RANK = 16
GLA_GATE_NORM = 16.0
GLA_CHUNK = 64
SWA_HEADS = 16
SWA_KV_HEADS = 4
SWA_GROUP = SWA_HEADS // SWA_KV_HEADS
SWA_HEAD_DIM = 64
SWA_WINDOW = 128
SWA_BLOCK = 128
assert SWA_WINDOW == SWA_BLOCK
ROPE_THETA = 500000.0
ROPE_DIM = SWA_HEAD_DIM // 4
ROPE_HALF = ROPE_DIM // 2
PEER_HEADS = 8
PEER_NKEYS = 128
PEER_QDIM = 256
PEER_TOPK = 16
PEER_HK = PEER_HEADS * PEER_TOPK
NORM_EPS = 1e-6

GLA_KW = GLA_HEADS * GLA_DK
GLA_VW = GLA_HEADS * GLA_DV
SWA_QW = SWA_HEADS * SWA_HEAD_DIM
SWA_KVW = SWA_KV_HEADS * SWA_HEAD_DIM
IN_SPLITS = (GLA_KW, GLA_KW, GLA_VW, GLA_VW, GLA_GATE_RANK, SWA_QW, SWA_KVW, SWA_KVW, D_MODEL, D_MODEL)
IN_OFFSETS = tuple(int(v) for v in np.cumsum((0,) + IN_SPLITS)[:-1])

LANES = 128
SUBLANES = 8
COL_GQ = 0
COL_GK = COL_GQ + GLA_KW
COL_GV = COL_GK + GLA_KW
COL_GR = COL_GV + GLA_VW
COL_SQ = COL_GR + GLA_VW
COL_GA = COL_SQ + SWA_QW
COL_GB = COL_GA + D_MODEL
COL_SK = COL_GB + D_MODEL
COL_SV = COL_SK + SWA_KVW
COL_GLR = COL_SV + SWA_KVW
GLR_PAD = 2 * LANES
PROJ_W = COL_GLR + GLR_PAD

VMEM_LIMIT = 48 * 1024 * 1024

F32 = jnp.float32
BF16 = jnp.bfloat16
HIGHEST = lax.Precision.HIGHEST


def _dot(a, b, precision=None):
    return jnp.dot(a, b, preferred_element_type=F32, precision=precision)


def _dot_nt(a, b):
    return lax.dot_general(a, b, (((1,), (1,)), ((), ())), preferred_element_type=F32)


def _gelu_exact(x):
    return 0.5 * x * (1.0 + lax.erf(x * (2.0 ** -0.5)))


def _dot_tn(a, b):
    return lax.dot_general(a, b, (((0,), (0,)), ((), ())), preferred_element_type=F32)


def _in_proj_kernel(x_ref, g_ref, w_ref, o_ref, xn_ref):
    @pl.when(pl.program_id(1) == 0)
    def _():
        x = x_ref[...]
        y = x * lax.rsqrt(jnp.mean(x * x, axis=-1, keepdims=True) + NORM_EPS) * g_ref[...]
        xn_ref[...] = y.astype(BF16)

    o_ref[...] = _dot(xn_ref[...], w_ref[...])


def in_proj(x2d, g, w_bf16, tm=1024, tn=2304):
    t, d = x2d.shape
    n = w_bf16.shape[1]
    return pl.pallas_call(
        _in_proj_kernel,
        grid=(t // tm, n // tn),
        in_specs=[pl.BlockSpec((tm, d), lambda i, j: (i, 0)),
                  pl.BlockSpec((1, d), lambda i, j: (0, 0)),
                  pl.BlockSpec((d, tn), lambda i, j: (0, j))],
        out_specs=pl.BlockSpec((tm, tn), lambda i, j: (i, j)),
        out_shape=jax.ShapeDtypeStruct((t, n), F32),
        scratch_shapes=[pltpu.VMEM((tm, d), BF16)],
        compiler_params=pltpu.CompilerParams(
            dimension_semantics=("parallel", "arbitrary"), vmem_limit_bytes=VMEM_LIMIT),
        name="in_proj",
    )(x2d, g.reshape(1, d), w_bf16)


def _relayout_w_in(w_in):
    seg = [w_in[:, o:o + s] for o, s in zip(IN_OFFSETS, IN_SPLITS)]
    gq, gk, gv, gr, glr, sq, sk, sv, ga, gb = seg
    glr_pad = jnp.pad(glr, ((0, 0), (0, GLR_PAD - GLA_GATE_RANK)))
    return jnp.concatenate([gq, gk, gv, gr, sq, ga, gb, sk, sv, glr_pad], axis=1).astype(BF16)


GLA_STEP = 512
GLA_HPS = 4


def _gla_kernel(q_ref, k_ref, v_ref, gr_ref, glr_ref, w2_ref, b2_ref, g_ref, cum_ref, o_ref, st_ref):
    @pl.when(pl.program_id(2) == 0)
    def _():
        st_ref[...] = jnp.zeros_like(st_ref)

    c, n = GLA_CHUNK, GLA_STEP
    row = lax.broadcasted_iota(jnp.int32, (c, c), 0)
    col = lax.broadcasted_iota(jnp.int32, (c, c), 1)
    causal = col <= row
    gain = g_ref[...]
    tril = cum_ref[...]
    glr = glr_ref[...]
    pre = []
    for hh in range(GLA_HPS):
        ks = slice(hh * GLA_DK, (hh + 1) * GLA_DK)
        q = q_ref[:, ks] * (GLA_DK ** -0.5)
        k = k_ref[:, ks]
        z = _dot(glr, w2_ref[:, ks], HIGHEST) + b2_ref[:, ks]
        log_a = jax.nn.log_sigmoid(z) / GLA_GATE_NORM
        parts = []
        for ci in range(n // c):
            bc = _dot(tril, log_a[ci * c:(ci + 1) * c], HIGHEST)
            parts.append((bc, jnp.broadcast_to(bc[c // 2:c // 2 + 1], bc.shape),
                          jnp.broadcast_to(bc[c - 1:c], bc.shape)))
        b, b_mid, b_last = (jnp.concatenate([p[i] for p in parts], axis=0) for i in range(3))
        pre.append(dict(q_in=(q * jnp.exp(b - b_mid)).astype(BF16), k_in=(k * jnp.exp(b_mid - b)).astype(BF16),
                        q_st=(q * jnp.exp(b)).astype(BF16), k_end=(k * jnp.exp(b_last - b)).astype(BF16),
                        decay=jnp.exp(b_last)))
    states = [st_ref[hh] for hh in range(GLA_HPS)]
    for ci in range(n // c):
        sl = slice(ci * c, (ci + 1) * c)
        for hh in range(GLA_HPS):
            p, st = pre[hh], states[hh]
            vs = pl.ds(hh * GLA_DV, GLA_DV)
            v = v_ref[pl.ds(ci * c, c), vs].astype(BF16)
            att = jnp.where(causal, _dot_nt(p["q_in"][sl], p["k_in"][sl]), 0.0)
            o = _dot(att.astype(BF16), v) + _dot_nt(p["q_st"][sl], st.astype(BF16))
            states[hh] = st * p["decay"][ci * c:ci * c + 1, :] + _dot_tn(v, p["k_end"][sl])
            y = o * lax.rsqrt(jnp.mean(o * o, axis=-1, keepdims=True) + NORM_EPS) * gain
            o_ref[pl.ds(ci * c, c), vs] = (y * jax.nn.silu(gr_ref[pl.ds(ci * c, c), vs])).astype(o_ref.dtype)
    for hh in range(GLA_HPS):
        st_ref[hh] = states[hh]


def _gla_cum_matrix():
    return jnp.asarray(np.tril(np.ones((GLA_CHUNK, GLA_CHUNK), np.float32)))


def gla(proj, w_gk2_pad, b_gk, gla_norm_g, bsz, seqlen):
    t = bsz * seqlen
    ns = seqlen // GLA_STEP
    rowmap = lambda b, h, s: b * ns + s
    kw, vw = GLA_HPS * GLA_DK, GLA_HPS * GLA_DV
    kq, kk = COL_GQ // kw, COL_GK // kw
    kv, kr = COL_GV // vw, COL_GR // vw
    kl = COL_GLR // LANES
    return pl.pallas_call(
        _gla_kernel,
        grid=(bsz, GLA_HEADS // GLA_HPS, ns),
        in_specs=[pl.BlockSpec((GLA_STEP, kw), lambda b, h, s: (rowmap(b, h, s), kq + h)),
                  pl.BlockSpec((GLA_STEP, kw), lambda b, h, s: (rowmap(b, h, s), kk + h)),
                  pl.BlockSpec((GLA_STEP, vw), lambda b, h, s: (rowmap(b, h, s), kv + h)),
                  pl.BlockSpec((GLA_STEP, vw), lambda b, h, s: (rowmap(b, h, s), kr + h)),
                  pl.BlockSpec((GLA_STEP, LANES), lambda b, h, s: (rowmap(b, h, s), kl)),
                  pl.BlockSpec((LANES, kw), lambda b, h, s: (0, h)),
                  pl.BlockSpec((1, kw), lambda b, h, s: (0, h)),
                  pl.BlockSpec((1, GLA_DV), lambda b, h, s: (0, 0)),
                  pl.BlockSpec((GLA_CHUNK, GLA_CHUNK), lambda b, h, s: (0, 0))],
        out_specs=pl.BlockSpec((GLA_STEP, vw), lambda b, h, s: (rowmap(b, h, s), h)),
        out_shape=jax.ShapeDtypeStruct((t, GLA_VW), BF16),
        scratch_shapes=[pltpu.VMEM((GLA_HPS, GLA_DV, GLA_DK), F32)],
        compiler_params=pltpu.CompilerParams(
            dimension_semantics=("parallel", "parallel", "arbitrary"), vmem_limit_bytes=VMEM_LIMIT),
        name="gla",
    )(proj, proj, proj, proj, proj, w_gk2_pad, b_gk.reshape(1, GLA_KW), gla_norm_g.reshape(1, GLA_DV),
      _gla_cum_matrix())


SWA_GW = SWA_GROUP * SWA_HEAD_DIM
assert SWA_KVW == SWA_GW


def _swa_kernel(sinks_ref, q_ref, k_ref, v_ref, cos_ref, sin_ref, qg_ref, kg_ref, bd_ref, rep_ref, o_ref,
                kprev_ref, vprev_ref):
    n = pl.program_id(1)

    @pl.when(n == 0)
    def _():
        kprev_ref[...] = jnp.zeros_like(kprev_ref)
        vprev_ref[...] = jnp.zeros_like(vprev_ref)

    L = SWA_BLOCK
    cos = cos_ref[...]
    sin = sin_ref[...]
    bd = bd_ref[...]
    lane = lax.broadcasted_iota(jnp.int32, (L, SWA_GW), 1)
    seg = lane // SWA_HEAD_DIM
    first_half = (lane % SWA_HEAD_DIM) < ROPE_HALF

    def norm_rope(x, gain):
        x2 = x * x
        hi = x2.astype(BF16)
        lo = (x2 - hi.astype(F32)).astype(BF16)
        ms = (_dot(hi, bd) + _dot(lo, bd)) * (1.0 / SWA_HEAD_DIM)
        y = x * lax.rsqrt(ms + NORM_EPS) * gain
        partner = jnp.where(first_half, pltpu.roll(y, SWA_GW - ROPE_HALF, 1), pltpu.roll(y, ROPE_HALF, 1))
        return y * cos + partner * sin

    rows = SWA_GROUP * L
    qi = lax.broadcasted_iota(jnp.int32, (rows, L), 0) % L
    ki = lax.broadcasted_iota(jnp.int32, (rows, L), 1)
    row_head = lax.broadcasted_iota(jnp.int32, (rows, 1), 0) // L
    mask_cur = ki <= qi
    mask_prev = ki > qi + jnp.where(n > 0, 0, L)
    neg_inf = F32(-jnp.inf)
    k_all = norm_rope(k_ref[...], kg_ref[...]).astype(BF16)
    v_all = v_ref[...].astype(BF16)
    for j in range(SWA_KV_HEADS):
        cs = pl.ds(j * SWA_GW, SWA_GW)
        q = (norm_rope(q_ref[:, cs], qg_ref[...]) * (SWA_HEAD_DIM ** -0.5)).astype(BF16)
        k_cur = _dot(k_all, rep_ref[j]).astype(BF16)
        v_cur = _dot(v_all, rep_ref[j]).astype(BF16)
        k_prev = kprev_ref[:, cs]
        v_prev = vprev_ref[:, cs]
        qs = jnp.concatenate([jnp.where(seg == g, q, jnp.zeros_like(q)) for g in range(SWA_GROUP)], axis=0)
        sink = jnp.zeros((rows, 1), F32)
        for g in range(SWA_GROUP):
            sink = jnp.where(row_head == g, sinks_ref[j * SWA_GROUP + g], sink)
        s_cur = jnp.where(mask_cur, _dot_nt(qs, k_cur), neg_inf)
        s_prev = jnp.where(mask_prev, _dot_nt(qs, k_prev), neg_inf)
        m = jnp.maximum(jnp.maximum(jnp.max(s_cur, axis=-1, keepdims=True),
                                    jnp.max(s_prev, axis=-1, keepdims=True)), sink)
        p_cur = jnp.exp(s_cur - m)
        p_prev = jnp.exp(s_prev - m)
        denom = (jnp.sum(p_cur, axis=-1, keepdims=True) + jnp.sum(p_prev, axis=-1, keepdims=True)
                 + jnp.exp(sink - m))
        og = (_dot(p_cur.astype(BF16), v_cur) + _dot(p_prev.astype(BF16), v_prev)) / denom
        acc = jnp.zeros((L, SWA_GW), F32)
        for g in range(SWA_GROUP):
            acc = acc + jnp.where(seg == g, og[g * L:(g + 1) * L], 0.0)
        o_ref[:, cs] = acc.astype(o_ref.dtype)
        kprev_ref[:, cs] = k_cur
        vprev_ref[:, cs] = v_cur


def _rope_tables(seqlen):
    pos = jnp.arange(seqlen, dtype=F32)
    inv_freq = ROPE_THETA ** (-jnp.arange(0, ROPE_DIM, 2, dtype=F32) / ROPE_DIM)
    ang = pos[:, None] * inv_freq[None, :]
    cos, sin = jnp.cos(ang), jnp.sin(ang)
    rest = SWA_HEAD_DIM - ROPE_DIM
    cos_h = jnp.concatenate([cos, cos, jnp.ones((seqlen, rest), F32)], axis=1)
    sin_h = jnp.concatenate([-sin, sin, jnp.zeros((seqlen, rest), F32)], axis=1)
    return jnp.tile(cos_h, (1, SWA_GROUP)), jnp.tile(sin_h, (1, SWA_GROUP))


def swa(proj, q_norm_g, k_norm_g, sinks, bsz, seqlen):
    t = bsz * seqlen
    nb = seqlen // SWA_BLOCK
    cos_t, sin_t = _rope_tables(seqlen)
    head = np.arange(SWA_GW) // SWA_HEAD_DIM
    blockdiag = jnp.asarray((head[:, None] == head[None, :]).astype(np.float32), dtype=BF16)
    src = np.arange(SWA_KVW)
    dst = np.arange(SWA_GW)
    rep = np.stack([(src[:, None] // SWA_HEAD_DIM == j) & (src[:, None] % SWA_HEAD_DIM == dst[None, :] % SWA_HEAD_DIM)
                    for j in range(SWA_KV_HEADS)]).astype(np.float32)
    rowmap = lambda b, n: b * nb + n
    cq, ck, cv = COL_SQ // SWA_QW, COL_SK // SWA_KVW, COL_SV // SWA_KVW
    grid_spec = pltpu.PrefetchScalarGridSpec(
        num_scalar_prefetch=1,
        grid=(bsz, nb),
        in_specs=[pl.BlockSpec((SWA_BLOCK, SWA_QW), lambda b, n, s: (rowmap(b, n), cq)),
                  pl.BlockSpec((SWA_BLOCK, SWA_KVW), lambda b, n, s: (rowmap(b, n), ck)),
                  pl.BlockSpec((SWA_BLOCK, SWA_KVW), lambda b, n, s: (rowmap(b, n), cv)),
                  pl.BlockSpec((SWA_BLOCK, SWA_GW), lambda b, n, s: (n, 0)),
                  pl.BlockSpec((SWA_BLOCK, SWA_GW), lambda b, n, s: (n, 0)),
                  pl.BlockSpec((1, SWA_GW), lambda b, n, s: (0, 0)),
                  pl.BlockSpec((1, SWA_GW), lambda b, n, s: (0, 0)),
                  pl.BlockSpec((SWA_GW, SWA_GW), lambda b, n, s: (0, 0)),
                  pl.BlockSpec((SWA_KV_HEADS, SWA_KVW, SWA_GW), lambda b, n, s: (0, 0, 0))],
        out_specs=pl.BlockSpec((SWA_BLOCK, SWA_QW), lambda b, n, s: (rowmap(b, n), 0)),
        scratch_shapes=[pltpu.VMEM((SWA_BLOCK, SWA_QW), BF16), pltpu.VMEM((SWA_BLOCK, SWA_QW), BF16)],
    )
    return pl.pallas_call(
        _swa_kernel,
        grid_spec=grid_spec,
        out_shape=jax.ShapeDtypeStruct((t, SWA_QW), BF16),
        compiler_params=pltpu.CompilerParams(
            dimension_semantics=("parallel", "arbitrary"), vmem_limit_bytes=VMEM_LIMIT),
        name="swa",
    )(sinks, proj, proj, proj, cos_t, sin_t,
      jnp.tile(q_norm_g, SWA_GROUP).reshape(1, SWA_GW), jnp.tile(k_norm_g, SWA_GROUP).reshape(1, SWA_GW),
      blockdiag, jnp.asarray(rep, dtype=BF16))


def _merge_kernel(oa_ref, ob_ref, ga_ref, gb_ref, x_ref, wa_ref, wb_ref, wo_ref, g_ref, wq_ref,
                  x1_ref, xn_ref, qp_ref):
    y_a = _dot(oa_ref[...], wa_ref[...])
    y_b = _dot(ob_ref[...], wb_ref[...])
    merged = jax.nn.sigmoid(ga_ref[...]) * y_a + jax.nn.sigmoid(gb_ref[...]) * y_b
    x1 = x_ref[...] + _dot(merged.astype(BF16), wo_ref[...])
    x1_ref[...] = x1
    xn = x1 * lax.rsqrt(jnp.mean(x1 * x1, axis=-1, keepdims=True) + NORM_EPS) * g_ref[...]
    xn_ref[...] = xn
    qp_ref[...] = _dot(xn.astype(BF16), wq_ref[...]).astype(qp_ref.dtype)


def merge(o_a, o_b, proj, x2d, w_a, w_b, w_o, norm_g, w_pq, tm=256):
    t, d = x2d.shape
    nq = w_pq.shape[1]
    ca, cb = COL_GA // D_MODEL, COL_GB // D_MODEL
    full = lambda shape: pl.BlockSpec(shape, lambda i: (0, 0))
    return pl.pallas_call(
        _merge_kernel,
        grid=(t // tm,),
        in_specs=[pl.BlockSpec((tm, GLA_VW), lambda i: (i, 0)),
                  pl.BlockSpec((tm, SWA_QW), lambda i: (i, 0)),
                  pl.BlockSpec((tm, d), lambda i: (i, ca)),
                  pl.BlockSpec((tm, d), lambda i: (i, cb)),
                  pl.BlockSpec((tm, d), lambda i: (i, 0)),
                  full((GLA_VW, d)), full((SWA_QW, d)), full((d, d)), full((1, d)), full((d, nq))],
        out_specs=[pl.BlockSpec((tm, d), lambda i: (i, 0)),
                   pl.BlockSpec((tm, d), lambda i: (i, 0)),
                   pl.BlockSpec((tm, nq), lambda i: (i, 0))],
        out_shape=[jax.ShapeDtypeStruct((t, d), F32),
                   jax.ShapeDtypeStruct((t, d), F32),
                   jax.ShapeDtypeStruct((t, nq), BF16)],
        compiler_params=pltpu.CompilerParams(
            dimension_semantics=("parallel",), vmem_limit_bytes=VMEM_LIMIT),
        name="merge",
    )(o_a, o_b, proj, proj, x2d, w_a.astype(BF16), w_b.astype(BF16), w_o.astype(BF16),
      norm_g.reshape(1, d), w_pq.astype(BF16))


PEER_TOK = 4 * LANES
PEER_HALF = PEER_QDIM // 2


def _top_rows(s, pos, payload=None):
    vals, rows = [], []
    for _ in range(PEER_TOPK):
        m = jnp.max(s, axis=0, keepdims=True)
        first = jnp.min(jnp.where(s == m, pos, F32(2 ** 24)), axis=0, keepdims=True)
        hit = pos == first
        vals.append(m)
        if payload is None:
            rows.append(first)
        else:
            rows.append(jnp.max(jnp.where(hit, payload, -1.0), axis=0, keepdims=True))
        s = jnp.where(hit, -jnp.inf, s)
    return jnp.concatenate(vals, axis=0), jnp.concatenate(rows, axis=0)


def _candidate_pieces():
    pieces = []
    single = []
    for a in range(PEER_TOPK):
        nb = PEER_TOPK // (a + 1)
        if nb >= 2:
            for b0 in range(0, nb, SUBLANES):
                pieces.append((a, 1, b0, SUBLANES))
        else:
            single.append(a)
    assert len(single) % SUBLANES == 0 and single == list(range(single[0], PEER_TOPK))
    for a0 in range(single[0], PEER_TOPK, SUBLANES):
        pieces.append((a0, SUBLANES, 0, 1))
    return pieces


def _route_subkeys(q_half, keys_half):
    key_id = lax.broadcasted_iota(jnp.int32, (PEER_NKEYS, PEER_TOK), 0).astype(F32)
    return _top_rows(_dot_nt(keys_half, q_half), key_id)


def _route_candidates(top_lo, top_hi):
    (v1, i1), (v2, i2) = top_lo, top_hi
    sub = lax.broadcasted_iota(jnp.int32, (SUBLANES, PEER_TOK), 0).astype(F32)
    pieces = _candidate_pieces()
    cpos = jnp.concatenate([(a0 + (sub if na > 1 else 0.0)) * PEER_TOPK + (b0 + (sub if nb > 1 else 0.0))
                            for a0, na, b0, nb in pieces], axis=0)
    cand = jnp.concatenate([v1[a0:a0 + na, :] + v2[b0:b0 + nb, :] for a0, na, b0, nb in pieces], axis=0)
    cidx = jnp.concatenate([i1[a0:a0 + na, :] * PEER_NKEYS + i2[b0:b0 + nb, :]
                            for a0, na, b0, nb in pieces], axis=0)
    return _top_rows(cand, cpos, cidx)


def _route_gates(best):
    e = jnp.exp(best - best[0:1, :])
    return e / jnp.sum(e, axis=0, keepdims=True)


def _route_head(q_lo, q_hi, keys_lo, keys_hi):
    best, eidx = _route_candidates(_route_subkeys(q_lo, keys_lo), _route_subkeys(q_hi, keys_hi))
    return eidx, _route_gates(best)


def _peer_topk_kernel(qp_ref, keys_ref, idx_ref, gate_ref):
    idx_rows, gate_rows = [], []
    for h in range(PEER_HEADS):
        eidx, gate_h = _route_head(qp_ref[:, pl.ds(2 * h * PEER_HALF, PEER_HALF)],
                                   qp_ref[:, pl.ds((2 * h + 1) * PEER_HALF, PEER_HALF)],
                                   keys_ref[2 * h], keys_ref[2 * h + 1])
        gate_rows.append(gate_h)
        idx_rows.append(eidx)
    gate = jnp.concatenate(gate_rows, axis=0)
    idx = jnp.concatenate(idx_rows, axis=0)
    gate_ref[...] = gate.T
    idx_ref[...] = idx.T.astype(jnp.int32)


def peer_topk(qp, sub_keys):
    t = qp.shape[0]
    keys = sub_keys.reshape(PEER_HEADS * 2, PEER_NKEYS, PEER_HALF).astype(BF16)
    return pl.pallas_call(
        _peer_topk_kernel,
        grid=(t // PEER_TOK,),
        in_specs=[pl.BlockSpec((PEER_TOK, PEER_HEADS * PEER_QDIM), lambda i: (i, 0)),
                  pl.BlockSpec((PEER_HEADS * 2, PEER_NKEYS, PEER_HALF), lambda i: (0, 0, 0))],
        out_specs=[pl.BlockSpec((PEER_TOK, PEER_HK), lambda i: (i, 0)),
                   pl.BlockSpec((PEER_TOK, PEER_HK), lambda i: (i, 0))],
        out_shape=[jax.ShapeDtypeStruct((t, PEER_HK), jnp.int32),
                   jax.ShapeDtypeStruct((t, PEER_HK), F32)],
        compiler_params=pltpu.CompilerParams(
            dimension_semantics=("arbitrary",), vmem_limit_bytes=VMEM_LIMIT),
        name="peer_topk",
    )(qp, keys)


PEXP_NBUF = 16
PEXP_SET = 4
PEXP_DMA_QUEUES = 2
PEXP_DT = D_MODEL // LANES
PEXP_ROW = 2 * PEXP_DT
PEXP_PITCH = PEXP_ROW + 1


def _peer_exp_kernel(idx_ref, nidx_ref, xn_ref, gate_ref, x1_ref, uv_ref, o_ref, *scratch):
    bufs, sem_ref = scratch[:PEXP_NBUF], scratch[PEXP_NBUF]
    sets = PEXP_NBUF // PEXP_SET
    step = pl.program_id(0)
    last_step = pl.num_programs(0) - 1

    def row_copy(src_row, slot, k):
        return pltpu.make_async_copy(uv_ref.at[src_row], bufs[slot].at[pl.ds(k * PEXP_PITCH, PEXP_ROW)],
                                     sem_ref.at[slot])

    def start(ids_ref, slot):
        for k in range(PEER_HK):
            row_copy(ids_ref[slot, k], slot, k).start(priority=k % PEXP_DMA_QUEUES)

    def wait(slot):
        for k in range(PEER_HK):
            row_copy(0, slot, k).wait()

    def hidden(slot):
        x = xn_ref[pl.ds(slot, 1), :].astype(BF16)
        h = jnp.zeros((1, PEER_HK), F32)
        for c in range(PEXP_DT):
            u_c = bufs[slot][pl.ds(c, PEER_HK, stride=PEXP_PITCH), :].astype(BF16)
            h = h + _dot_nt(x[:, c * LANES:(c + 1) * LANES], u_c)
        return (gate_ref[pl.ds(slot, 1), :] * _gelu_exact(h)).astype(BF16)

    def project(slot, w):
        out = [_dot(w, bufs[slot][pl.ds(PEXP_DT + c, PEER_HK, stride=PEXP_PITCH), :].astype(BF16))
               for c in range(PEXP_DT)]
        o_ref[pl.ds(slot, 1), :] = x1_ref[pl.ds(slot, 1), :] + jnp.concatenate(out, axis=1)

    @pl.when(step == 0)
    def _():
        for slot in range(PEXP_NBUF - PEXP_SET):
            start(idx_ref, slot)

    for s in range(sets):
        slots = range(s * PEXP_SET, (s + 1) * PEXP_SET)
        for slot in slots:
            wait(slot)
        if s == 0:
            for slot in range((sets - 1) * PEXP_SET, PEXP_NBUF):
                start(idx_ref, slot)
        else:
            for slot in range((s - 1) * PEXP_SET, s * PEXP_SET):
                start(nidx_ref, slot)
        ws = [hidden(slot) for slot in slots]
        for slot, w in zip(slots, ws):
            project(slot, w)

    @pl.when(step == last_step)
    def _():
        for slot in range(PEXP_NBUF - PEXP_SET):
            wait(slot)


def peer_exp(idx, xn, gate, x1, uv):
    t, d = xn.shape
    steps = t // PEXP_NBUF
    tok_block = lambda width: pl.BlockSpec((PEXP_NBUF, width), lambda i: (i, 0))
    return pl.pallas_call(
        _peer_exp_kernel,
        grid=(steps,),
        in_specs=[pl.BlockSpec((PEXP_NBUF, PEER_HK), lambda i: (i, 0), memory_space=pltpu.SMEM),
                  pl.BlockSpec((PEXP_NBUF, PEER_HK), lambda i: (jnp.minimum(i + 1, steps - 1), 0),
                               memory_space=pltpu.SMEM),
                  tok_block(d), tok_block(PEER_HK), tok_block(d),
                  pl.BlockSpec(memory_space=pl.ANY)],
        out_specs=tok_block(d),
        out_shape=jax.ShapeDtypeStruct((t, d), F32),
        scratch_shapes=[pltpu.VMEM((PEER_HK * PEXP_PITCH, LANES), F32)] * PEXP_NBUF
                       + [pltpu.SemaphoreType.DMA((PEXP_NBUF,))],
        compiler_params=pltpu.CompilerParams(
            dimension_semantics=("arbitrary",), vmem_limit_bytes=VMEM_LIMIT),
        name="peer_exp",
    )(idx, idx, xn, gate, x1, uv)


def kernel(x, norm_mix_g, w_in, w_gk2, b_gk, gla_norm_g, q_norm_g, k_norm_g, attn_sinks, w_branch_a, w_branch_b, w_out, norm_ffn_g, w_peer_q, peer_sub_keys, peer_u, peer_v):
    bsz, seqlen, d = x.shape
    t = bsz * seqlen
    depth = w_in.shape[0]
    x2d = x.reshape(t, d)
    for l in range(depth):
        proj = in_proj(x2d, norm_mix_g[l], _relayout_w_in(w_in[l]))
        w_gk2_pad = jnp.pad(w_gk2[l], ((0, LANES - GLA_GATE_RANK), (0, 0)))
        o_a = gla(proj, w_gk2_pad, b_gk[l], gla_norm_g[l], bsz, seqlen)
        o_b = swa(proj, q_norm_g[l], k_norm_g[l], attn_sinks[l], bsz, seqlen)
        x1, xn, qp = merge(o_a, o_b, proj, x2d, w_branch_a[l], w_branch_b[l], w_out[l],
                           norm_ffn_g[l], w_peer_q[l])
        idx, gate = peer_topk(qp, peer_sub_keys[l])
        uv = jnp.concatenate([peer_u[l], peer_v[l]], axis=1).reshape(-1, PEXP_ROW, LANES)
        x2d = peer_exp(idx, xn, gate, x1, uv)
    return x2d.reshape(bsz, seqlen, d)
```

```python
import jax
import jax.numpy as jnp
import numpy as np
from jax import lax
from jax.experimental import pallas as pl
from jax.experimental.pallas import tpu as pltpu

D_MODEL = 1024
GLA_HEADS = 4
GLA_DK = 128
GLA_DV = 256
GLA_GATE_RANK = 16
GLA_GATE_NORM = 16.0
GLA_CHUNK = 64
SWA_HEADS = 16
SWA_KV_HEADS = 4
SWA_GROUP = SWA_HEADS // SWA_KV_HEADS
SWA_HEAD_DIM = 64
SWA_WINDOW = 128
SWA_BLOCK = 128
assert SWA_WINDOW == SWA_BLOCK
ROPE_THETA = 500000.0
ROPE_DIM = SWA_HEAD_DIM // 4
ROPE_HALF = ROPE_DIM // 2
PEER_HEADS = 8
PEER_NKEYS = 128
PEER_QDIM = 256
PEER_TOPK = 16
PEER_HK = PEER_HEADS * PEER_TOPK
NORM_EPS = 1e-6

GLA_KW = GLA_HEADS * GLA_DK
GLA_VW = GLA_HEADS * GLA_DV
SWA_QW = SWA_HEADS * SWA_HEAD_DIM
SWA_KVW = SWA_KV_HEADS * SWA_HEAD_DIM
IN_SPLITS = (GLA_KW, GLA_KW, GLA_VW, GLA_VW, GLA_GATE_RANK, SWA_QW, SWA_KVW, SWA_KVW, D_MODEL, D_MODEL)
IN_OFFSETS = tuple(int(v) for v in np.cumsum((0,) + IN_SPLITS)[:-1])

LANES = 128
SUBLANES = 8
COL_GQ = 0
COL_GK = COL_GQ + GLA_KW
COL_GV = COL_GK + GLA_KW
COL_GR = COL_GV + GLA_VW
COL_SQ = COL_GR + GLA_VW
COL_GA = COL_SQ + SWA_QW
COL_GB = COL_GA + D_MODEL
COL_SK = COL_GB + D_MODEL
COL_SV = COL_SK + SWA_KVW
COL_GLR = COL_SV + SWA_KVW
GLR_PAD = 2 * LANES
PROJ_W = COL_GLR + GLR_PAD

VMEM_LIMIT = 48 * 1024 * 1024

F32 = jnp.float32
BF16 = jnp.bfloat16
HIGHEST = lax.Precision.HIGHEST


def _dot(a, b, precision=None):
    return jnp.dot(a, b, preferred_element_type=F32, precision=precision)


def _dot_nt(a, b):
    return lax.dot_general(a, b, (((1,), (1,)), ((), ())), preferred_element_type=F32)


def _gelu_exact(x):
    return 0.5 * x * (1.0 + lax.erf(x * (2.0 ** -0.5)))


def _dot_tn(a, b):
    return lax.dot_general(a, b, (((0,), (0,)), ((), ())), preferred_element_type=F32)


def _in_proj_kernel(x_ref, g_ref, w_ref, o_ref, xn_ref):
    @pl.when(pl.program_id(1) == 0)
    def _():
        x = x_ref[...]
        y = x * lax.rsqrt(jnp.mean(x * x, axis=-1, keepdims=True) + NORM_EPS) * g_ref[...]
        xn_ref[...] = y.astype(BF16)

    o_ref[...] = _dot(xn_ref[...], w_ref[...])


def in_proj(x2d, g, w_bf16, tm=1024, tn=2304):
    t, d = x2d.shape
    n = w_bf16.shape[1]
    return pl.pallas_call(
        _in_proj_kernel,
        grid=(t // tm, n // tn),
        in_specs=[pl.BlockSpec((tm, d), lambda i, j: (i, 0)),
                  pl.BlockSpec((1, d), lambda i, j: (0, 0)),
                  pl.BlockSpec((d, tn), lambda i, j: (0, j))],
        out_specs=pl.BlockSpec((tm, tn), lambda i, j: (i, j)),
        out_shape=jax.ShapeDtypeStruct((t, n), F32),
        scratch_shapes=[pltpu.VMEM((tm, d), BF16)],
        compiler_params=pltpu.CompilerParams(
            dimension_semantics=("parallel", "arbitrary"), vmem_limit_bytes=VMEM_LIMIT),
        name="in_proj",
    )(x2d, g.reshape(1, d), w_bf16)


def _relayout_w_in(w_in):
    seg = [w_in[:, o:o + s] for o, s in zip(IN_OFFSETS, IN_SPLITS)]
    gq, gk, gv, gr, glr, sq, sk, sv, ga, gb = seg
    glr_pad = jnp.pad(glr, ((0, 0), (0, GLR_PAD - GLA_GATE_RANK)))
    return jnp.concatenate([gq, gk, gv, gr, sq, ga, gb, sk, sv, glr_pad], axis=1).astype(BF16)


GLA_STEP = 512
GLA_HPS = 4


def _gla_kernel(q_ref, k_ref, v_ref, gr_ref, glr_ref, w2_ref, b2_ref, g_ref, cum_ref, o_ref, st_ref):
    @pl.when(pl.program_id(2) == 0)
    def _():
        st_ref[...] = jnp.zeros_like(st_ref)

    c, n = GLA_CHUNK, GLA_STEP
    row = lax.broadcasted_iota(jnp.int32, (c, c), 0)
    col = lax.broadcasted_iota(jnp.int32, (c, c), 1)
    causal = col <= row
    gain = g_ref[...]
    tril = cum_ref[...]
    glr = glr_ref[...]
    pre = []
    for hh in range(GLA_HPS):
        ks = slice(hh * GLA_DK, (hh + 1) * GLA_DK)
        q = q_ref[:, ks] * (GLA_DK ** -0.5)
        k = k_ref[:, ks]
        z = _dot(glr, w2_ref[:, ks], HIGHEST) + b2_ref[:, ks]
        log_a = jax.nn.log_sigmoid(z) / GLA_GATE_NORM
        parts = []
        for ci in range(n // c):
            bc = _dot(tril, log_a[ci * c:(ci + 1) * c], HIGHEST)
            parts.append((bc, jnp.broadcast_to(bc[c // 2:c // 2 + 1], bc.shape),
                          jnp.broadcast_to(bc[c - 1:c], bc.shape)))
        b, b_mid, b_last = (jnp.concatenate([p[i] for p in parts], axis=0) for i in range(3))
        pre.append(dict(q_in=(q * jnp.exp(b - b_mid)).astype(BF16), k_in=(k * jnp.exp(b_mid - b)).astype(BF16),
                        q_st=(q * jnp.exp(b)).astype(BF16), k_end=(k * jnp.exp(b_last - b)).astype(BF16),
                        decay=jnp.exp(b_last)))
    states = [st_ref[hh] for hh in range(GLA_HPS)]
    for ci in range(n // c):
        sl = slice(ci * c, (ci + 1) * c)
        for hh in range(GLA_HPS):
            p, st = pre[hh], states[hh]
            vs = pl.ds(hh * GLA_DV, GLA_DV)
            v = v_ref[pl.ds(ci * c, c), vs].astype(BF16)
            att = jnp.where(causal, _dot_nt(p["q_in"][sl], p["k_in"][sl]), 0.0)
            o = _dot(att.astype(BF16), v) + _dot_nt(p["q_st"][sl], st.astype(BF16))
            states[hh] = st * p["decay"][ci * c:ci * c + 1, :] + _dot_tn(v, p["k_end"][sl])
            y = o * lax.rsqrt(jnp.mean(o * o, axis=-1, keepdims=True) + NORM_EPS) * gain
            o_ref[pl.ds(ci * c, c), vs] = (y * jax.nn.silu(gr_ref[pl.ds(ci * c, c), vs])).astype(o_ref.dtype)
    for hh in range(GLA_HPS):
        st_ref[hh] = states[hh]


def _gla_cum_matrix():
    return jnp.asarray(np.tril(np.ones((GLA_CHUNK, GLA_CHUNK), np.float32)))


def gla(proj, w_gk2_pad, b_gk, gla_norm_g, bsz, seqlen):
    t = bsz * seqlen
    ns = seqlen // GLA_STEP
    rowmap = lambda b, h, s: b * ns + s
    kw, vw = GLA_HPS * GLA_DK, GLA_HPS * GLA_DV
    kq, kk = COL_GQ // kw, COL_GK // kw
    kv, kr = COL_GV // vw, COL_GR // vw
    kl = COL_GLR // LANES
    return pl.pallas_call(
        _gla_kernel,
        grid=(bsz, GLA_HEADS // GLA_HPS, ns),
        in_specs=[pl.BlockSpec((GLA_STEP, kw), lambda b, h, s: (rowmap(b, h, s), kq + h)),
                  pl.BlockSpec((GLA_STEP, kw), lambda b, h, s: (rowmap(b, h, s), kk + h)),
                  pl.BlockSpec((GLA_STEP, vw), lambda b, h, s: (rowmap(b, h, s), kv + h)),
                  pl.BlockSpec((GLA_STEP, vw), lambda b, h, s: (rowmap(b, h, s), kr + h)),
                  pl.BlockSpec((GLA_STEP, LANES), lambda b, h, s: (rowmap(b, h, s), kl)),
                  pl.BlockSpec((LANES, kw), lambda b, h, s: (0, h)),
                  pl.BlockSpec((1, kw), lambda b, h, s: (0, h)),
                  pl.BlockSpec((1, GLA_DV), lambda b, h, s: (0, 0)),
                  pl.BlockSpec((GLA_CHUNK, GLA_CHUNK), lambda b, h, s: (0, 0))],
        out_specs=pl.BlockSpec((GLA_STEP, vw), lambda b, h, s: (rowmap(b, h, s), h)),
        out_shape=jax.ShapeDtypeStruct((t, GLA_VW), BF16),
        scratch_shapes=[pltpu.VMEM((GLA_HPS, GLA_DV, GLA_DK), F32)],
        compiler_params=pltpu.CompilerParams(
            dimension_semantics=("parallel", "parallel", "arbitrary"), vmem_limit_bytes=VMEM_LIMIT),
        name="gla",
    )(proj, proj, proj, proj, proj, w_gk2_pad, b_gk.reshape(1, GLA_KW), gla_norm_g.reshape(1, GLA_DV),
      _gla_cum_matrix())


SWA_GW = SWA_GROUP * SWA_HEAD_DIM
assert SWA_KVW == SWA_GW


def _swa_kernel(sinks_ref, q_ref, k_ref, v_ref, cos_ref, sin_ref, qg_ref, kg_ref, bd_ref, rep_ref, o_ref,
                kprev_ref, vprev_ref):
    n = pl.program_id(1)

    @pl.when(n == 0)
    def _():
        kprev_ref[...] = jnp.zeros_like(kprev_ref)
        vprev_ref[...] = jnp.zeros_like(vprev_ref)

    L = SWA_BLOCK
    cos = cos_ref[...]
    sin = sin_ref[...]
    bd = bd_ref[...]
    lane = lax.broadcasted_iota(jnp.int32, (L, SWA_GW), 1)
    seg = lane // SWA_HEAD_DIM
    first_half = (lane % SWA_HEAD_DIM) < ROPE_HALF

    def norm_rope(x, gain):
        x2 = x * x
        hi = x2.astype(BF16)
        lo = (x2 - hi.astype(F32)).astype(BF16)
        ms = (_dot(hi, bd) + _dot(lo, bd)) * (1.0 / SWA_HEAD_DIM)
        y = x * lax.rsqrt(ms + NORM_EPS) * gain
        partner = jnp.where(first_half, pltpu.roll(y, SWA_GW - ROPE_HALF, 1), pltpu.roll(y, ROPE_HALF, 1))
        return y * cos + partner * sin

    rows = SWA_GROUP * L
    qi = lax.broadcasted_iota(jnp.int32, (rows, L), 0) % L
    ki = lax.broadcasted_iota(jnp.int32, (rows, L), 1)
    row_head = lax.broadcasted_iota(jnp.int32, (rows, 1), 0) // L
    mask_cur = ki <= qi
    mask_prev = ki > qi + jnp.where(n > 0, 0, L)
    neg_inf = F32(-jnp.inf)
    k_all = norm_rope(k_ref[...], kg_ref[...]).astype(BF16)
    v_all = v_ref[...].astype(BF16)
    for j in range(SWA_KV_HEADS):
        cs = pl.ds(j * SWA_GW, SWA_GW)
        q = (norm_rope(q_ref[:, cs], qg_ref[...]) * (SWA_HEAD_DIM ** -0.5)).astype(BF16)
        k_cur = _dot(k_all, rep_ref[j]).astype(BF16)
        v_cur = _dot(v_all, rep_ref[j]).astype(BF16)
        k_prev = kprev_ref[:, cs]
        v_prev = vprev_ref[:, cs]
        qs = jnp.concatenate([jnp.where(seg == g, q, jnp.zeros_like(q)) for g in range(SWA_GROUP)], axis=0)
        sink = jnp.zeros((rows, 1), F32)
        for g in range(SWA_GROUP):
            sink = jnp.where(row_head == g, sinks_ref[j * SWA_GROUP + g], sink)
        s_cur = jnp.where(mask_cur, _dot_nt(qs, k_cur), neg_inf)
        s_prev = jnp.where(mask_prev, _dot_nt(qs, k_prev), neg_inf)
        m = jnp.maximum(jnp.maximum(jnp.max(s_cur, axis=-1, keepdims=True),
                                    jnp.max(s_prev, axis=-1, keepdims=True)), sink)
        p_cur = jnp.exp(s_cur - m)
        p_prev = jnp.exp(s_prev - m)
        denom = (jnp.sum(p_cur, axis=-1, keepdims=True) + jnp.sum(p_prev, axis=-1, keepdims=True)
                 + jnp.exp(sink - m))
        og = (_dot(p_cur.astype(BF16), v_cur) + _dot(p_prev.astype(BF16), v_prev)) / denom
        acc = jnp.zeros((L, SWA_GW), F32)
        for g in range(SWA_GROUP):
            acc = acc + jnp.where(seg == g, og[g * L:(g + 1) * L], 0.0)
        o_ref[:, cs] = acc.astype(o_ref.dtype)
        kprev_ref[:, cs] = k_cur
        vprev_ref[:, cs] = v_cur


def _rope_tables(seqlen):
    pos = jnp.arange(seqlen, dtype=F32)
    inv_freq = ROPE_THETA ** (-jnp.arange(0, ROPE_DIM, 2, dtype=F32) / ROPE_DIM)
    ang = pos[:, None] * inv_freq[None, :]
    cos, sin = jnp.cos(ang), jnp.sin(ang)
    rest = SWA_HEAD_DIM - ROPE_DIM
    cos_h = jnp.concatenate([cos, cos, jnp.ones((seqlen, rest), F32)], axis=1)
    sin_h = jnp.concatenate([-sin, sin, jnp.zeros((seqlen, rest), F32)], axis=1)
    return jnp.tile(cos_h, (1, SWA_GROUP)), jnp.tile(sin_h, (1, SWA_GROUP))


def swa(proj, q_norm_g, k_norm_g, sinks, bsz, seqlen):
    t = bsz * seqlen
    nb = seqlen // SWA_BLOCK
    cos_t, sin_t = _rope_tables(seqlen)
    head = np.arange(SWA_GW) // SWA_HEAD_DIM
    blockdiag = jnp.asarray((head[:, None] == head[None, :]).astype(np.float32), dtype=BF16)
    src = np.arange(SWA_KVW)
    dst = np.arange(SWA_GW)
    rep = np.stack([(src[:, None] // SWA_HEAD_DIM == j) & (src[:, None] % SWA_HEAD_DIM == dst[None, :] % SWA_HEAD_DIM)
                    for j in range(SWA_KV_HEADS)]).astype(np.float32)
    rowmap = lambda b, n: b * nb + n
    cq, ck, cv = COL_SQ // SWA_QW, COL_SK // SWA_KVW, COL_SV // SWA_KVW
    grid_spec = pltpu.PrefetchScalarGridSpec(
        num_scalar_prefetch=1,
        grid=(bsz, nb),
        in_specs=[pl.BlockSpec((SWA_BLOCK, SWA_QW), lambda b, n, s: (rowmap(b, n), cq)),
                  pl.BlockSpec((SWA_BLOCK, SWA_KVW), lambda b, n, s: (rowmap(b, n), ck)),
                  pl.BlockSpec((SWA_BLOCK, SWA_KVW), lambda b, n, s: (rowmap(b, n), cv)),
                  pl.BlockSpec((SWA_BLOCK, SWA_GW), lambda b, n, s: (n, 0)),
                  pl.BlockSpec((SWA_BLOCK, SWA_GW), lambda b, n, s: (n, 0)),
                  pl.BlockSpec((1, SWA_GW), lambda b, n, s: (0, 0)),
                  pl.BlockSpec((1, SWA_GW), lambda b, n, s: (0, 0)),
                  pl.BlockSpec((SWA_GW, SWA_GW), lambda b, n, s: (0, 0)),
                  pl.BlockSpec((SWA_KV_HEADS, SWA_KVW, SWA_GW), lambda b, n, s: (0, 0, 0))],
        out_specs=pl.BlockSpec((SWA_BLOCK, SWA_QW), lambda b, n, s: (rowmap(b, n), 0)),
        scratch_shapes=[pltpu.VMEM((SWA_BLOCK, SWA_QW), BF16), pltpu.VMEM((SWA_BLOCK, SWA_QW), BF16)],
    )
    return pl.pallas_call(
        _swa_kernel,
        grid_spec=grid_spec,
        out_shape=jax.ShapeDtypeStruct((t, SWA_QW), BF16),
        compiler_params=pltpu.CompilerParams(
            dimension_semantics=("parallel", "arbitrary"), vmem_limit_bytes=VMEM_LIMIT),
        name="swa",
    )(sinks, proj, proj, proj, cos_t, sin_t,
      jnp.tile(q_norm_g, SWA_GROUP).reshape(1, SWA_GW), jnp.tile(k_norm_g, SWA_GROUP).reshape(1, SWA_GW),
      blockdiag, jnp.asarray(rep, dtype=BF16))


def _merge_kernel(oa_ref, ob_ref, ga_ref, gb_ref, x_ref, wa_ref, wb_ref, wo_ref, g_ref, wq_ref, keys_ref,
                  x1_ref, xn_ref, idx_ref, gate_ref):
    y_a = _dot(oa_ref[...], wa_ref[...])
    y_b = _dot(ob_ref[...], wb_ref[...])
    merged = jax.nn.sigmoid(ga_ref[...]) * y_a + jax.nn.sigmoid(gb_ref[...]) * y_b
    x1 = x_ref[...] + _dot(merged.astype(BF16), wo_ref[...])
    x1_ref[...] = x1
    xn = x1 * lax.rsqrt(jnp.mean(x1 * x1, axis=-1, keepdims=True) + NORM_EPS) * g_ref[...]
    xn_ref[...] = xn
    qp = _dot(xn.astype(BF16), wq_ref[...]).astype(BF16)
    idx_rows, gate_rows = [], []
    for h in range(PEER_HEADS):
        eidx, gate_h = _route_head(qp[:, 2 * h * PEER_HALF:(2 * h + 1) * PEER_HALF],
                                   qp[:, (2 * h + 1) * PEER_HALF:(2 * h + 2) * PEER_HALF],
                                   keys_ref[2 * h], keys_ref[2 * h + 1])
        gate_rows.append(gate_h)
        idx_rows.append(eidx)
    gate_ref[...] = jnp.concatenate(gate_rows, axis=0).T
    idx_ref[...] = jnp.concatenate(idx_rows, axis=0).T.astype(jnp.int32)


def merge(o_a, o_b, proj, x2d, w_a, w_b, w_o, norm_g, w_pq, sub_keys):
    t, d = x2d.shape
    nq = w_pq.shape[1]
    tm = PEER_TOK
    ca, cb = COL_GA // D_MODEL, COL_GB // D_MODEL
    keys = sub_keys.reshape(PEER_HEADS * 2, PEER_NKEYS, PEER_HALF).astype(BF16)
    full = lambda shape: pl.BlockSpec(shape, lambda i: (0,) * len(shape))
    return pl.pallas_call(
        _merge_kernel,
        grid=(t // tm,),
        in_specs=[pl.BlockSpec((tm, GLA_VW), lambda i: (i, 0)),
                  pl.BlockSpec((tm, SWA_QW), lambda i: (i, 0)),
                  pl.BlockSpec((tm, d), lambda i: (i, ca)),
                  pl.BlockSpec((tm, d), lambda i: (i, cb)),
                  pl.BlockSpec((tm, d), lambda i: (i, 0)),
                  full((GLA_VW, d)), full((SWA_QW, d)), full((d, d)), full((1, d)), full((d, nq)),
                  full((PEER_HEADS * 2, PEER_NKEYS, PEER_HALF))],
        out_specs=[pl.BlockSpec((tm, d), lambda i: (i, 0)),
                   pl.BlockSpec((tm, d), lambda i: (i, 0)),
                   pl.BlockSpec((tm, PEER_HK), lambda i: (i, 0)),
                   pl.BlockSpec((tm, PEER_HK), lambda i: (i, 0))],
        out_shape=[jax.ShapeDtypeStruct((t, d), F32),
                   jax.ShapeDtypeStruct((t, d), F32),
                   jax.ShapeDtypeStruct((t, PEER_HK), jnp.int32),
                   jax.ShapeDtypeStruct((t, PEER_HK), F32)],
        compiler_params=pltpu.CompilerParams(
            dimension_semantics=("parallel",), vmem_limit_bytes=VMEM_LIMIT),
        name="merge",
    )(o_a, o_b, proj, proj, x2d, w_a.astype(BF16), w_b.astype(BF16), w_o.astype(BF16),
      norm_g.reshape(1, d), w_pq.astype(BF16), keys)


PEER_TOK = 2 * LANES
PEER_HALF = PEER_QDIM // 2


def _top_rows(s, pos, payload=None):
    vals, rows = [], []
    for _ in range(PEER_TOPK):
        m = jnp.max(s, axis=0, keepdims=True)
        first = jnp.min(jnp.where(s == m, pos, F32(2 ** 24)), axis=0, keepdims=True)
        hit = pos == first
        vals.append(m)
        if payload is None:
            rows.append(first)
        else:
            rows.append(jnp.max(jnp.where(hit, payload, -1.0), axis=0, keepdims=True))
        s = jnp.where(hit, -jnp.inf, s)
    return jnp.concatenate(vals, axis=0), jnp.concatenate(rows, axis=0)


def _candidate_pieces():
    pieces = []
    single = []
    for a in range(PEER_TOPK):
        nb = PEER_TOPK // (a + 1)
        if nb >= 2:
            for b0 in range(0, nb, SUBLANES):
                pieces.append((a, 1, b0, SUBLANES))
        else:
            single.append(a)
    assert len(single) % SUBLANES == 0 and single == list(range(single[0], PEER_TOPK))
    for a0 in range(single[0], PEER_TOPK, SUBLANES):
        pieces.append((a0, SUBLANES, 0, 1))
    return pieces


def _route_subkeys(q_half, keys_half):
    key_id = lax.broadcasted_iota(jnp.int32, (PEER_NKEYS, PEER_TOK), 0).astype(F32)
    return _top_rows(_dot_nt(keys_half, q_half), key_id)


def _route_candidates(top_lo, top_hi):
    (v1, i1), (v2, i2) = top_lo, top_hi
    sub = lax.broadcasted_iota(jnp.int32, (SUBLANES, PEER_TOK), 0).astype(F32)
    pieces = _candidate_pieces()
    cpos = jnp.concatenate([(a0 + (sub if na > 1 else 0.0)) * PEER_TOPK + (b0 + (sub if nb > 1 else 0.0))
                            for a0, na, b0, nb in pieces], axis=0)
    cand = jnp.concatenate([v1[a0:a0 + na, :] + v2[b0:b0 + nb, :] for a0, na, b0, nb in pieces], axis=0)
    cidx = jnp.concatenate([i1[a0:a0 + na, :] * PEER_NKEYS + i2[b0:b0 + nb, :]
                            for a0, na, b0, nb in pieces], axis=0)
    return _top_rows(cand, cpos, cidx)


def _route_gates(best):
    e = jnp.exp(best - best[0:1, :])
    return e / jnp.sum(e, axis=0, keepdims=True)


def _route_head(q_lo, q_hi, keys_lo, keys_hi):
    best, eidx = _route_candidates(_route_subkeys(q_lo, keys_lo), _route_subkeys(q_hi, keys_hi))
    return eidx, _route_gates(best)


def _peer_topk_kernel(qp_ref, keys_ref, idx_ref, gate_ref):
    idx_rows, gate_rows = [], []
    for h in range(PEER_HEADS):
        eidx, gate_h = _route_head(qp_ref[:, pl.ds(2 * h * PEER_HALF, PEER_HALF)],
                                   qp_ref[:, pl.ds((2 * h + 1) * PEER_HALF, PEER_HALF)],
                                   keys_ref[2 * h], keys_ref[2 * h + 1])
        gate_rows.append(gate_h)
        idx_rows.append(eidx)
    gate = jnp.concatenate(gate_rows, axis=0)
    idx = jnp.concatenate(idx_rows, axis=0)
    gate_ref[...] = gate.T
    idx_ref[...] = idx.T.astype(jnp.int32)


def peer_topk(qp, sub_keys):
    t = qp.shape[0]
    keys = sub_keys.reshape(PEER_HEADS * 2, PEER_NKEYS, PEER_HALF).astype(BF16)
    return pl.pallas_call(
        _peer_topk_kernel,
        grid=(t // PEER_TOK,),
        in_specs=[pl.BlockSpec((PEER_TOK, PEER_HEADS * PEER_QDIM), lambda i: (i, 0)),
                  pl.BlockSpec((PEER_HEADS * 2, PEER_NKEYS, PEER_HALF), lambda i: (0, 0, 0))],
        out_specs=[pl.BlockSpec((PEER_TOK, PEER_HK), lambda i: (i, 0)),
                   pl.BlockSpec((PEER_TOK, PEER_HK), lambda i: (i, 0))],
        out_shape=[jax.ShapeDtypeStruct((t, PEER_HK), jnp.int32),
                   jax.ShapeDtypeStruct((t, PEER_HK), F32)],
        compiler_params=pltpu.CompilerParams(
            dimension_semantics=("arbitrary",), vmem_limit_bytes=VMEM_LIMIT),
        name="peer_topk",
    )(qp, keys)


PEXP_NBUF = 16
PEXP_SET = 4
PEXP_DMA_QUEUES = 2
PEXP_DT = D_MODEL // LANES
PEXP_ROW = 2 * PEXP_DT
PEXP_PITCH = PEXP_ROW + 1


def _peer_exp_kernel(idx_ref, nidx_ref, xn_ref, gate_ref, x1_ref, uv_ref, o_ref, *scratch):
    bufs, sem_ref = scratch[:PEXP_NBUF], scratch[PEXP_NBUF]
    sets = PEXP_NBUF // PEXP_SET
    step = pl.program_id(0)
    last_step = pl.num_programs(0) - 1

    def row_copy(src_row, slot, k):
        return pltpu.make_async_copy(uv_ref.at[src_row], bufs[slot].at[pl.ds(k * PEXP_PITCH, PEXP_ROW)],
                                     sem_ref.at[slot])

    def start(ids_ref, slot):
        for k in range(PEER_HK):
            row_copy(ids_ref[slot, k], slot, k).start(priority=k % PEXP_DMA_QUEUES)

    def wait(slot):
        for k in range(PEER_HK):
            row_copy(0, slot, k).wait()

    def hidden(slot):
        x = xn_ref[pl.ds(slot, 1), :].astype(BF16)
        h = jnp.zeros((1, PEER_HK), F32)
        for c in range(PEXP_DT):
            u_c = bufs[slot][pl.ds(c, PEER_HK, stride=PEXP_PITCH), :].astype(BF16)
            h = h + _dot_nt(x[:, c * LANES:(c + 1) * LANES], u_c)
        return (gate_ref[pl.ds(slot, 1), :] * _gelu_exact(h)).astype(BF16)

    def project(slot, w):
        out = [_dot(w, bufs[slot][pl.ds(PEXP_DT + c, PEER_HK, stride=PEXP_PITCH), :].astype(BF16))
               for c in range(PEXP_DT)]
        o_ref[pl.ds(slot, 1), :] = x1_ref[pl.ds(slot, 1), :] + jnp.concatenate(out, axis=1)

    @pl.when(step == 0)
    def _():
        for slot in range(PEXP_NBUF - PEXP_SET):
            start(idx_ref, slot)

    for s in range(sets):
        slots = range(s * PEXP_SET, (s + 1) * PEXP_SET)
        for slot in slots:
            wait(slot)
        if s == 0:
            for slot in range((sets - 1) * PEXP_SET, PEXP_NBUF):
                start(idx_ref, slot)
        else:
            for slot in range((s - 1) * PEXP_SET, s * PEXP_SET):
                start(nidx_ref, slot)
        ws = [hidden(slot) for slot in slots]
        for slot, w in zip(slots, ws):
            project(slot, w)

    @pl.when(step == last_step)
    def _():
        for slot in range(PEXP_NBUF - PEXP_SET):
            wait(slot)


def peer_exp(idx, xn, gate, x1, uv):
    t, d = xn.shape
    steps = t // PEXP_NBUF
    tok_block = lambda width: pl.BlockSpec((PEXP_NBUF, width), lambda i: (i, 0))
    return pl.pallas_call(
        _peer_exp_kernel,
        grid=(steps,),
        in_specs=[pl.BlockSpec((PEXP_NBUF, PEER_HK), lambda i: (i, 0), memory_space=pltpu.SMEM),
                  pl.BlockSpec((PEXP_NBUF, PEER_HK), lambda i: (jnp.minimum(i + 1, steps - 1), 0),
                               memory_space=pltpu.SMEM),
                  tok_block(d), tok_block(PEER_HK), tok_block(d),
                  pl.BlockSpec(memory_space=pl.ANY)],
        out_specs=tok_block(d),
        out_shape=jax.ShapeDtypeStruct((t, d), F32),
        scratch_shapes=[pltpu.VMEM((PEER_HK * PEXP_PITCH, LANES), F32)] * PEXP_NBUF
                       + [pltpu.SemaphoreType.DMA((PEXP_NBUF,))],
        compiler_params=pltpu.CompilerParams(
            dimension_semantics=("arbitrary",), vmem_limit_bytes=VMEM_LIMIT),
        name="peer_exp",
    )(idx, idx, xn, gate, x1, uv)


def kernel(x, norm_mix_g, w_in, w_gk2, b_gk, gla_norm_g, q_norm_g, k_norm_g, attn_sinks, w_branch_a, w_branch_b, w_out, norm_ffn_g, w_peer_q, peer_sub_keys, peer_u, peer_v):
    bsz, seqlen, d = x.shape
    t = bsz * seqlen
    depth = w_in.shape[0]
    x2d = x.reshape(t, d)
    for l in range(depth):
        proj = in_proj(x2d, norm_mix_g[l], _relayout_w_in(w_in[l]))
        w_gk2_pad = jnp.pad(w_gk2[l], ((0, LANES - GLA_GATE_RANK), (0, 0)))
        o_a = gla(proj, w_gk2_pad, b_gk[l], gla_norm_g[l], bsz, seqlen)
        o_b = swa(proj, q_norm_g[l], k_norm_g[l], attn_sinks[l], bsz, seqlen)
        x1, xn, idx, gate = merge(o_a, o_b, proj, x2d, w_branch_a[l], w_branch_b[l], w_out[l],
                                  norm_ffn_g[l], w_peer_q[l], peer_sub_keys[l])
        uv = jnp.concatenate([peer_u[l], peer_v[l]], axis=1).reshape(-1, PEXP_ROW, LANES)
        x2d = peer_exp(idx, xn, gate, x1, uv)
    return x2d.reshape(bsz, seqlen, d)
```
